```python
import jax
import jax.numpy as jnp
from jax import lax
import numpy as np

D_MODEL = 4096
BATCH = 1
SEQ = 16384
DEPTH = 4

CTX_LEN = 256
GRID_W = 64
CHUNK = 128
ROWS_PER_CHUNK = CHUNK // GRID_W
A_WIDTH = 1024
A_GROUPS = 4
A_GROUP_DIM = A_WIDTH // A_GROUPS
M_HEADS = 4
M_DK = 128
M_DV = 256
M_WIDTH = M_HEADS * M_DV
MLSTM_CHUNK = 128
N_DIRS = 2
IF_WIDTH = N_DIRS * 2 * M_HEADS
GATE_SOFTCAP = 15.0
FORGET_BIAS = 3.0
IN_WIDTHS = (A_WIDTH, A_WIDTH, M_HEADS * M_DK, M_HEADS * M_DK, M_WIDTH, M_WIDTH, IF_WIDTH, D_MODEL, D_MODEL)
N_IN = 2 * A_WIDTH + 2 * M_HEADS * M_DK + 2 * M_WIDTH + IF_WIDTH + 2 * D_MODEL
D_FF = 4096
N_EXPERTS = 8
TOP_K = 2
D_EXPERT = 512
N_DENSE = (DEPTH + 1) // 2
N_MOE = DEPTH // 2
N_MOD = 6
EPS = 1e-6

kernel_name = "hybrid_gmlp_mlstm_moe_dit"


def _rmsnorm(x, g):
    x32 = x.astype(jnp.float32)
    y = x32 * lax.rsqrt(jnp.mean(x32 * x32, axis=-1, keepdims=True) + EPS)
    return (y * g.astype(jnp.float32)).astype(x.dtype)


def _layernorm(x, g):
    x32 = x.astype(jnp.float32)
    xc = x32 - jnp.mean(x32, axis=-1, keepdims=True)
    y = xc * lax.rsqrt(jnp.mean(xc * xc, axis=-1, keepdims=True) + EPS)
    return (y * g.astype(jnp.float32)).astype(x.dtype)


def _adaln(cond, w, b):
    mod = jax.nn.silu(cond) @ w + b
    return [m[:, None, :] for m in jnp.split(mod, N_MOD, axis=-1)]


def _modulate(h, shift, scale):
    return h * (1.0 + scale) + shift


def _chunk_mlp(u, v, n_chunks, g_norm, w_s, b_s):
    bsz, n, _ = v.shape
    v = _layernorm(v, g_norm).reshape(bsz, n_chunks, CHUNK, A_GROUPS, A_GROUP_DIM)
    s = jnp.einsum("gij,bcjgd->bcigd", w_s, v) + b_s.T[:, :, None]
    return u * s.reshape(bsz, n, A_WIDTH)


def _heads(a, dh):
    bsz, n, _ = a.shape
    return a.reshape(bsz, n, M_HEADS, dh).transpose(0, 2, 1, 3).astype(jnp.float32)


def _mlstm_gates(pre_if, bias):
    pre = pre_if.astype(jnp.float32) + bias.astype(jnp.float32)
    pre = GATE_SOFTCAP * jnp.tanh(pre / GATE_SOFTCAP)
    bsz, n, _ = pre.shape
    pre = pre.reshape(bsz, n, N_DIRS, 2, M_HEADS).transpose(2, 3, 0, 4, 1)
    return pre[:, 0], jax.nn.log_sigmoid(pre[:, 1])


def _mlstm_inputs(p, bias):
    q = _heads(p[2], M_DK) * (M_DK ** -0.5)
    k = _heads(p[3], M_DK)
    v = _heads(p[4], M_DV)
    ig, lf = _mlstm_gates(p[6], bias)
    return q, k, v, ig, lf


def _mlstm_scan(q, k, v, ig, lf, state):
    bsz, nh, n, _ = q.shape
    nc = n // MLSTM_CHUNK

    def to_chunks(a):
        a = a.reshape(a.shape[:2] + (nc, MLSTM_CHUNK) + a.shape[3:])
        return jnp.moveaxis(a, 2, 0)

    tril = jnp.tril(jnp.ones((MLSTM_CHUNK, MLSTM_CHUNK), dtype=bool))

    def step(carry, xs):
        c_st, n_st, m_st = carry
        qc, kc, vc, igc, lfc = xs
        b = jnp.cumsum(lfc, axis=-1)
        log_d = jnp.where(tril, b[..., :, None] - b[..., None, :] + igc[..., None, :], -jnp.inf)
        m_inter = b + m_st[..., None]
        m_row = jnp.maximum(m_inter, jnp.max(log_d, axis=-1))
        s = jnp.einsum("bhik,bhjk->bhij", qc, kc) * jnp.exp(log_d - m_row[..., None])
        w_inter = jnp.exp(m_inter - m_row)
        num = jnp.einsum("bhij,bhjv->bhiv", s, vc) + w_inter[..., None] * jnp.einsum("bhik,bhkv->bhiv", qc, c_st)
        den = jnp.sum(s, axis=-1) + w_inter * jnp.einsum("bhik,bhk->bhi", qc, n_st)
        h = num / jnp.maximum(jnp.abs(den), jnp.exp(-m_row))[..., None]
        b_last = b[..., -1]
        log_w = b_last[..., None] - b + igc
        m_new = jnp.maximum(b_last + m_st, jnp.max(log_w, axis=-1))
        w = jnp.exp(log_w - m_new[..., None])
        decay = jnp.exp(b_last + m_st - m_new)
        c_new = decay[..., None, None] * c_st + jnp.einsum("bhl,bhlk,bhlv->bhkv", w, kc, vc)
        n_new = decay[..., None] * n_st + jnp.einsum("bhl,bhlk->bhk", w, kc)
        return (c_new, n_new, m_new), h

    state, h = lax.scan(step, state, (to_chunks(q), to_chunks(k), to_chunks(v), to_chunks(ig), to_chunks(lf)))
    return jnp.moveaxis(h, 0, 2).reshape(bsz, nh, n, v.shape[-1]), state


def _mlstm_bidir(q, k, v, ig, lf, states):
    rev = lambda a: jnp.flip(a, axis=2)
    h_f, st_f = _mlstm_scan(q, k, v, ig[0], lf[0], states[0])
    h_b, st_b = _mlstm_scan(rev(q), rev(k), rev(v), jnp.flip(ig[1], -1), jnp.flip(lf[1], -1), states[1])
    return h_f + rev(h_b), (st_f, st_b)


def _mlstm_out(h, o, g):
    h = h * lax.rsqrt(jnp.mean(h * h, axis=-1, keepdims=True) + EPS)
    bsz, _, n, _ = h.shape
    h = h.transpose(0, 2, 1, 3).reshape(bsz, n, M_WIDTH) * g.astype(jnp.float32)
    return (h * jax.nn.sigmoid(o.astype(jnp.float32))).astype(o.dtype)


def _merge_branches(p, hm, n_chunks, sgu_norm, sgu_w, sgu_b, mh_norm, w_br_a, w_br_b, w_out):
    u, v, _, _, _, o, _, g_a, g_b = p
    y_a = _chunk_mlp(jax.nn.gelu(u), jax.nn.gelu(v), n_chunks, sgu_norm, sgu_w, sgu_b) @ w_br_a
    y_b = _mlstm_out(hm, o, mh_norm) @ w_br_b
    return (jax.nn.sigmoid(g_a) * y_a + jax.nn.sigmoid(g_b) * y_b) @ w_out


def _swiglu(h, w13, w2):
    g, u = jnp.split(h @ w13, 2, axis=-1)
    return (jax.nn.silu(g) * u) @ w2


def _moe_swiglu(h, router, w13, w2):
    logits = jnp.einsum("bnd,de->bne", h.astype(jnp.float32), router.astype(jnp.float32))
    top_v, top_i = lax.top_k(logits, TOP_K)
    top_w = jax.nn.softmax(top_v, axis=-1)
    gates = jnp.sum(jax.nn.one_hot(top_i, N_EXPERTS, dtype=jnp.float32) * top_w[..., None], axis=-2)
    g, u = jnp.split(jnp.einsum("bnd,edf->bnef", h, w13), 2, axis=-1)
    act = jax.nn.silu(g) * u * gates[..., None].astype(h.dtype)
    return jnp.einsum("bnef,efd->bnd", act, w2)


def setup_inputs(seed: int = 0) -> dict:
    key = jax.random.key(seed)
    ks = jax.random.split(key, 24)
    f32 = jnp.float32
    d = D_MODEL

    def nrm(k, shape, scale=1.0):
        return jax.random.normal(k, shape, f32) * scale

    def gain(k, shape):
        return 1.0 + 0.02 * jax.random.normal(k, shape, f32)

    if_base = jnp.zeros((N_DIRS, 2, M_HEADS), f32).at[:, 1, :].set(FORGET_BIAS).reshape(IF_WIDTH)
    return {
        "x": nrm(ks[0], (BATCH, SEQ, d)),
        "c": nrm(ks[1], (BATCH, d)),
        "ctx": nrm(ks[2], (BATCH, CTX_LEN, d)),
        "c_ctx": nrm(ks[3], (d,)),
        "ada_w": nrm(ks[4], (DEPTH, d, N_MOD * d), 0.5 * d ** -0.5),
        "ada_b": nrm(ks[5], (DEPTH, N_MOD * d), 0.02),
        "norm_mix": gain(ks[6], (DEPTH, d)),
        "w_in": nrm(ks[7], (DEPTH, d, N_IN), d ** -0.5),
        "sgu_norm": gain(ks[8], (DEPTH, A_WIDTH)),
        "sgu_w": nrm(ks[9], (DEPTH, A_GROUPS, CHUNK, CHUNK), CHUNK ** -0.5),
        "sgu_b": gain(ks[10], (DEPTH, A_GROUPS, CHUNK)),
        "gate_bias": if_base + nrm(ks[11], (DEPTH, IF_WIDTH), 0.1),
        "mh_norm": gain(ks[12], (DEPTH, M_WIDTH)),
        "w_br_a": nrm(ks[13], (DEPTH, A_WIDTH, d), A_WIDTH ** -0.5),
        "w_br_b": nrm(ks[14], (DEPTH, M_WIDTH, d), M_WIDTH ** -0.5),
        "w_out": nrm(ks[15], (DEPTH, d, d), d ** -0.5),
        "norm_ffn": gain(ks[16], (DEPTH, d)),
        "dense_w13": nrm(ks[17], (N_DENSE, d, 2 * D_FF), d ** -0.5),
        "dense_w2": nrm(ks[18], (N_DENSE, D_FF, d), D_FF ** -0.5),
        "moe_router": nrm(ks[19], (N_MOE, d, N_EXPERTS), d ** -0.5),
        "moe_w13": nrm(ks[20], (N_MOE, N_EXPERTS, d, 2 * D_EXPERT), d ** -0.5),
        "moe_w2": nrm(ks[21], (N_MOE, N_EXPERTS, D_EXPERT, d), D_EXPERT ** -0.5),
        "final_norm": gain(ks[22], (d,)),
    }


def reference(x, c, ctx, c_ctx, ada_w, ada_b, norm_mix, w_in, sgu_norm, sgu_w, sgu_b, gate_bias, mh_norm,
              w_br_a, w_br_b, w_out, norm_ffn, dense_w13, dense_w2, moe_router, moe_w13, moe_w2, final_norm):
    bsz = x.shape[0]
    rows = x.shape[1] // GRID_W
    n_chunks_lat = rows // ROWS_PER_CHUNK
    n_chunks_ctx = ctx.shape[1] // CHUNK
    splits = np.cumsum(IN_WIDTHS)[:-1].tolist()
    zero = (jnp.zeros((bsz, M_HEADS, M_DK, M_DV), jnp.float32),
            jnp.zeros((bsz, M_HEADS, M_DK), jnp.float32),
            jnp.zeros((bsz, M_HEADS), jnp.float32))

    def channel_mixer(l, h):
        if l % 2 == 0:
            return _swiglu(h, dense_w13[l // 2], dense_w2[l // 2])
        return _moe_swiglu(h, moe_router[l // 2], moe_w13[l // 2], moe_w2[l // 2])

    xc = ctx
    for l in range(DEPTH):
        last = l == DEPTH - 1
        sh_m, sc_m, g_m, sh_f, sc_f, g_f = _adaln(c, ada_w[l], ada_b[l])
        csh_m, csc_m, cg_m, csh_f, csc_f, cg_f = _adaln(c_ctx[None, :], ada_w[l], ada_b[l])
        p_lat = jnp.split(_modulate(_rmsnorm(x, norm_mix[l]), sh_m, sc_m) @ w_in[l], splits, axis=-1)
        p_ctx = jnp.split(_modulate(_rmsnorm(xc, norm_mix[l]), csh_m, csc_m) @ w_in[l], splits, axis=-1)
        hm_ctx, ctx_states = _mlstm_bidir(*_mlstm_inputs(p_ctx, gate_bias[l]), (zero, zero))
        hm_lat, _ = _mlstm_bidir(*_mlstm_inputs(p_lat, gate_bias[l]), ctx_states)
        branch = (sgu_norm[l], sgu_w[l], sgu_b[l], mh_norm[l], w_br_a[l], w_br_b[l], w_out[l])
        x = x + g_m * _merge_branches(p_lat, hm_lat, n_chunks_lat, *branch)
        x = x + g_f * channel_mixer(l, _modulate(_rmsnorm(x, norm_ffn[l]), sh_f, sc_f))
        if not last:
            xc = xc + cg_m * _merge_branches(p_ctx, hm_ctx, n_chunks_ctx, *branch)
            xc = xc + cg_f * channel_mixer(l, _modulate(_rmsnorm(xc, norm_ffn[l]), csh_f, csc_f))
    return _rmsnorm(x, final_norm)
```

```python
import functools

import jax
import jax.numpy as jnp
from jax import lax
from jax.experimental import pallas as pl
from jax.experimental.pallas import tpu as pltpu

F32 = jnp.float32
BF16 = jnp.bfloat16

CHUNK = 128
A_GROUPS = 4
M_HEADS = 4
M_DK = 128
M_DV = 256
N_DIRS = 2
IF_WIDTH = N_DIRS * 2 * M_HEADS
GATE_SOFTCAP = 15.0
N_EXPERTS = 8
N_MOD = 6
EPS = 1e-6

LANES = 128
V7X_VMEM_BYTES = 64 * 1024 * 1024
VMEM_LIMIT_CAP = V7X_VMEM_BYTES - 6 * 1024 * 1024


def _params(n_grid_dims, block_bytes):
    need = 2 * block_bytes + block_bytes // 2 + 4 * 1024 * 1024
    return pltpu.CompilerParams(
        dimension_semantics=("arbitrary",) * n_grid_dims,
        vmem_limit_bytes=int(min(max(need, 16 * 1024 * 1024), VMEM_LIMIT_CAP)))


def _nbytes(shape, dtype):
    n = 1
    for s in shape:
        n *= s
    return n * jnp.dtype(dtype).itemsize


def _tile(n, pref):
    if n <= pref:
        return n
    t = pref
    while n % t:
        t -= LANES
    assert t > 0
    return t


def _sigmoid(x):
    return 1.0 / (1.0 + jnp.exp(-x))


def _gelu(x):
    return 0.5 * x * (1.0 + jnp.tanh(0.7978845608028654 * (x + 0.044715 * (x * x * x))))


def _log_sigmoid(x):
    return jnp.minimum(x, 0.0) - jnp.log1p(jnp.exp(-jnp.abs(x)))


def _adaln_kernel(cond_ref, w_ref, b_ref, o_ref):
    c = cond_ref[...]
    s = (c * _sigmoid(c)).astype(BF16)
    acc = jnp.dot(s, w_ref[...].astype(BF16), preferred_element_type=F32)
    o_ref[...] = acc + b_ref[...]


def _adaln(cond, ada_w, ada_b):
    n_layers, d, n = ada_w.shape
    r = cond.shape[0]
    tn = _tile(n, 512)
    blk = _nbytes((d, tn), F32) + _nbytes((r, d), F32) + 2 * _nbytes((r, tn), F32)
    return pl.pallas_call(
        _adaln_kernel,
        grid=(n_layers, n // tn),
        in_specs=[pl.BlockSpec((r, d), lambda l, j: (0, 0)),
                  pl.BlockSpec((None, d, tn), lambda l, j: (l, 0, j)),
                  pl.BlockSpec((None, 1, tn), lambda l, j: (l, 0, j))],
        out_specs=pl.BlockSpec((None, r, tn), lambda l, j: (l, 0, j)),
        out_shape=jax.ShapeDtypeStruct((n_layers, r, n), F32),
        compiler_params=_params(2, blk), name="adaln",
    )(cond, ada_w, ada_b.reshape(n_layers, 1, n))


def _rms(x, g):
    return x * lax.rsqrt(jnp.mean(x * x, axis=-1, keepdims=True) + EPS) * g


def _norm_mod_kernel(x_ref, g_ref, shift_ref, scale_ref, o_ref):
    h = _rms(x_ref[...], g_ref[...]) * (1.0 + scale_ref[...]) + shift_ref[...]
    o_ref[...] = h.astype(o_ref.dtype)


def _norm_mod_router_kernel(x_ref, g_ref, shift_ref, scale_ref, r_ref, o_ref, gate_ref):
    h = (_rms(x_ref[...], g_ref[...]) * (1.0 + scale_ref[...]) + shift_ref[...]).astype(BF16)
    o_ref[...] = h
    logits = jnp.dot(h, r_ref[...], preferred_element_type=F32)
    lane = lax.broadcasted_iota(jnp.int32, logits.shape, 1)
    logits = jnp.where(lane < N_EXPERTS, logits, -jnp.inf)
    m1 = jnp.max(logits, axis=-1, keepdims=True)
    i1 = jnp.min(jnp.where(logits == m1, lane, LANES), axis=-1, keepdims=True)
    rest = jnp.where(lane == i1, -jnp.inf, logits)
    m2 = jnp.max(rest, axis=-1, keepdims=True)
    i2 = jnp.min(jnp.where(rest == m2, lane, LANES), axis=-1, keepdims=True)
    e2 = jnp.exp(m2 - m1)
    w1 = 1.0 / (1.0 + e2)
    w2 = e2 / (1.0 + e2)
    gate_ref[...] = jnp.where(lane == i1, w1, 0.0) + jnp.where(lane == i2, w2, 0.0)


def _norm_mod(x, g, shift, scale, router=None, out_dtype=BF16):
    m, d = x.shape
    tm = _tile(m, 256)
    row = pl.BlockSpec((1, d), lambda i: (0, 0))
    in_specs = [pl.BlockSpec((tm, d), lambda i: (i, 0)), row, row, row]
    blk = _nbytes((tm, d), F32) * 3
    if router is None:
        return pl.pallas_call(
            _norm_mod_kernel, grid=(m // tm,), in_specs=in_specs,
            out_specs=pl.BlockSpec((tm, d), lambda i: (i, 0)),
            out_shape=jax.ShapeDtypeStruct((m, d), out_dtype),
            compiler_params=_params(1, blk), name="norm_mod",
        )(x, g, shift, scale)
    return pl.pallas_call(
        _norm_mod_router_kernel, grid=(m // tm,),
        in_specs=in_specs + [pl.BlockSpec((d, LANES), lambda i: (0, 0))],
        out_specs=[pl.BlockSpec((tm, d), lambda i: (i, 0)), pl.BlockSpec((tm, LANES), lambda i: (i, 0))],
        out_shape=[jax.ShapeDtypeStruct((m, d), BF16), jax.ShapeDtypeStruct((m, LANES), F32)],
        compiler_params=_params(1, blk + _nbytes((d, LANES), BF16)), name="norm_mod_router",
    )(x, g, shift, scale, router)


def _mm_act_kernel(x_ref, w_ref, o_ref, *, act):
    acc = jnp.dot(x_ref[...], w_ref[...], preferred_element_type=F32)
    if act == "gelu":
        acc = _gelu(acc)
    elif act == "sigmoid":
        acc = _sigmoid(acc)
    o_ref[...] = acc.astype(o_ref.dtype)


def _mm_act(x, w, act, out_dtype, tm_pref=1024, tn_pref=1024):
    m, k = x.shape
    n = w.shape[1]
    tm, tn = _tile(m, tm_pref), _tile(n, tn_pref)
    blk = _nbytes((tm, k), BF16) + _nbytes((k, tn), BF16) + _nbytes((tm, tn), out_dtype) + _nbytes((tm, tn), F32) // 2
    return pl.pallas_call(
        functools.partial(_mm_act_kernel, act=act),
        grid=(m // tm, n // tn),
        in_specs=[pl.BlockSpec((tm, k), lambda i, j: (i, 0)), pl.BlockSpec((k, tn), lambda i, j: (0, j))],
        out_specs=pl.BlockSpec((tm, tn), lambda i, j: (i, j)),
        out_shape=jax.ShapeDtypeStruct((m, n), out_dtype),
        compiler_params=_params(2, blk), name="mm_" + act,
    )(x, w)


def _mm_colscale_kernel(x_ref, w_ref, s_ref, o_ref):
    acc = jnp.dot(x_ref[...], w_ref[...], preferred_element_type=F32)
    o_ref[...] = (acc * s_ref[...]).astype(o_ref.dtype)


def _mm_colscale(x, w, col_scale, out_dtype, tm_pref=1024, tn_pref=1024):
    m, k = x.shape
    n = w.shape[1]
    tm, tn = _tile(m, tm_pref), _tile(n, tn_pref)
    blk = _nbytes((tm, k), BF16) + _nbytes((k, tn), BF16) + _nbytes((tm, tn), out_dtype) + _nbytes((tm, tn), F32) // 2
    return pl.pallas_call(
        _mm_colscale_kernel,
        grid=(m // tm, n // tn),
        in_specs=[pl.BlockSpec((tm, k), lambda i, j: (i, 0)), pl.BlockSpec((k, tn), lambda i, j: (0, j)),
                  pl.BlockSpec((1, tn), lambda i, j: (0, j))],
        out_specs=pl.BlockSpec((tm, tn), lambda i, j: (i, j)),
        out_shape=jax.ShapeDtypeStruct((m, n), out_dtype),
        compiler_params=_params(2, blk), name="mm_colscale",
    )(x, w, col_scale)


def _mm_residual_kernel(z_ref, w_ref, x_ref, g_ref, o_ref):
    acc = jnp.dot(z_ref[...], w_ref[...], preferred_element_type=F32)
    o_ref[...] = x_ref[...] + g_ref[...] * acc


def _mm_residual(z, w, x, gate, tm_pref=1024, tn_pref=512):
    m, k = z.shape
    n = w.shape[1]
    tm, tn = _tile(m, tm_pref), _tile(n, tn_pref)
    blk = _nbytes((tm, k), BF16) + _nbytes((k, tn), BF16) + 3 * _nbytes((tm, tn), F32)
    return pl.pallas_call(
        _mm_residual_kernel,
        grid=(m // tm, n // tn),
        in_specs=[pl.BlockSpec((tm, k), lambda i, j: (i, 0)), pl.BlockSpec((k, tn), lambda i, j: (0, j)),
                  pl.BlockSpec((tm, tn), lambda i, j: (i, j)), pl.BlockSpec((1, tn), lambda i, j: (0, j))],
        out_specs=pl.BlockSpec((tm, tn), lambda i, j: (i, j)),
        out_shape=jax.ShapeDtypeStruct((m, n), F32),
        compiler_params=_params(2, blk), name="mm_residual",
    )(z, w, x, gate)


def _mm_merge_kernel(ta_ref, tb_ref, wa_ref, wb_ref, ga_ref, gb_ref, o_ref):
    ya = jnp.dot(ta_ref[...], wa_ref[...], preferred_element_type=F32)
    yb = jnp.dot(tb_ref[...], wb_ref[...], preferred_element_type=F32)
    o_ref[...] = (ga_ref[...] * ya + gb_ref[...] * yb).astype(o_ref.dtype)


def _mm_merge(t, wa, wb, sig, tm_pref=1024, tn_pref=1024):
    m = t.shape[0]
    ka, n = wa.shape
    assert wb.shape == wa.shape and t.shape[1] == 2 * ka
    ko = sig.shape[1] - 2 * n
    tm, tn = _tile(m, tm_pref), _tile(n, tn_pref)
    off_a = ko // tn
    off_b = (ko + n) // tn
    assert off_a * tn == ko and off_b * tn == ko + n
    blk = 2 * _nbytes((tm, ka), BF16) + 2 * _nbytes((ka, tn), BF16) + 4 * _nbytes((tm, tn), F32)
    return pl.pallas_call(
        _mm_merge_kernel,
        grid=(m // tm, n // tn),
        in_specs=[pl.BlockSpec((tm, ka), lambda i, j: (i, 0)), pl.BlockSpec((tm, ka), lambda i, j: (i, 1)),
                  pl.BlockSpec((ka, tn), lambda i, j: (0, j)), pl.BlockSpec((ka, tn), lambda i, j: (0, j)),
                  pl.BlockSpec((tm, tn), lambda i, j: (i, j + off_a)),
                  pl.BlockSpec((tm, tn), lambda i, j: (i, j + off_b))],
        out_specs=pl.BlockSpec((tm, tn), lambda i, j: (i, j)),
        out_shape=jax.ShapeDtypeStruct((m, n), BF16),
        compiler_params=_params(2, blk), name="mm_merge",
    )(t, t, wa, wb, sig, sig)


def _ffn_up_kernel(h_ref, wg_ref, wu_ref, o_ref):
    h = h_ref[...]
    g = jnp.dot(h, wg_ref[...], preferred_element_type=F32)
    u = jnp.dot(h, wu_ref[...], preferred_element_type=F32)
    o_ref[...] = (g * _sigmoid(g) * u).astype(o_ref.dtype)


def _ffn_up_moe_kernel(h_ref, wg_ref, wu_ref, gate_ref, o_ref):
    h = h_ref[...]
    g = jnp.dot(h, wg_ref[...], preferred_element_type=F32)
    u = jnp.dot(h, wu_ref[...], preferred_element_type=F32)
    gates = gate_ref[...]
    lane = lax.broadcasted_iota(jnp.int32, gates.shape, 1)
    gate = jnp.sum(jnp.where(lane == pl.program_id(1), gates, 0.0), axis=-1, keepdims=True)
    o_ref[...] = (g * _sigmoid(g) * u * gate).astype(o_ref.dtype)


def _ffn_up_dense(h, w13, tm_pref=1024, tn_pref=512):
    m, k = h.shape
    f = w13.shape[1] // 2
    tm, tn = _tile(m, tm_pref), _tile(f, tn_pref)
    nb = f // tn
    blk = _nbytes((tm, k), BF16) + 2 * _nbytes((k, tn), BF16) + 3 * _nbytes((tm, tn), F32)
    return pl.pallas_call(
        _ffn_up_kernel,
        grid=(m // tm, nb),
        in_specs=[pl.BlockSpec((tm, k), lambda i, j: (i, 0)), pl.BlockSpec((k, tn), lambda i, j: (0, j)),
                  pl.BlockSpec((k, tn), lambda i, j: (0, j + nb))],
        out_specs=pl.BlockSpec((tm, tn), lambda i, j: (i, j)),
        out_shape=jax.ShapeDtypeStruct((m, f), BF16),
        compiler_params=_params(2, blk), name="ffn_up_dense",
    )(h, w13, w13)


def _ffn_up_moe(h, w13, gates, tm_pref=1024):
    m, k = h.shape
    n_exp, _, two_de = w13.shape
    de = two_de // 2
    tm = _tile(m, tm_pref)
    blk = _nbytes((tm, k), BF16) + 2 * _nbytes((k, de), BF16) + 3 * _nbytes((tm, de), F32) + _nbytes((tm, LANES), F32)
    return pl.pallas_call(
        _ffn_up_moe_kernel,
        grid=(m // tm, n_exp),
        in_specs=[pl.BlockSpec((tm, k), lambda i, j: (i, 0)), pl.BlockSpec((None, k, de), lambda i, j: (j, 0, 0)),
                  pl.BlockSpec((None, k, de), lambda i, j: (j, 0, 1)), pl.BlockSpec((tm, LANES), lambda i, j: (i, 0))],
        out_specs=pl.BlockSpec((tm, de), lambda i, j: (i, j)),
        out_shape=jax.ShapeDtypeStruct((m, n_exp * de), BF16),
        compiler_params=_params(2, blk), name="ffn_up_moe",
    )(h, w13, w13, gates)


def _split3(x):
    hi = x.astype(BF16)
    r1 = x - hi.astype(F32)
    mid = r1.astype(BF16)
    lo = (r1 - mid.astype(F32)).astype(BF16)
    return hi, mid, lo


def _mlstm_kernel(qkv_f_ref, qkv_b_ref, pre_f_ref, pre_b_ref, bias_ref, c0_ref, n0_ref, m0_ref,
                  hf_ref, hb_ref, c_ref, n_ref, m_ref):
    @pl.when(pl.program_id(0) == 0)
    def _():
        c_ref[...] = c0_ref[...]
        n_ref[...] = n0_ref[...]
        m_ref[...] = m0_ref[...]

    rows = lax.broadcasted_iota(jnp.int32, (CHUNK, CHUNK), 0)
    cols = lax.broadcasted_iota(jnp.int32, (CHUNK, CHUNK), 1)
    for d, (qkv_ref, pre_ref, out_ref) in enumerate(((qkv_f_ref, pre_f_ref, hf_ref), (qkv_b_ref, pre_b_ref, hb_ref))):
        mask = (rows >= cols) if d == 0 else (rows <= cols)
        tri = jnp.where(mask, 1.0, 0.0).astype(BF16)
        a = pre_ref[...] + bias_ref[...]
        a = GATE_SOFTCAP * jnp.tanh(a / GATE_SOFTCAP)
        lf = _log_sigmoid(a)
        hi, mid, lo = _split3(lf)
        b_all = (jnp.dot(tri, hi, preferred_element_type=F32) + jnp.dot(tri, mid, preferred_element_type=F32)
                 + jnp.dot(tri, lo, preferred_element_type=F32))
        a_t = a.T
        b_t = b_all.T
        last = CHUNK - 1 if d == 0 else 0
        for hd in range(M_HEADS):
            ci = d * 2 * M_HEADS + hd
            cf = ci + M_HEADS
            r = d * M_HEADS + hd
            ig_col, ig_row = a[:, ci:ci + 1], a_t[ci:ci + 1, :]
            b_col, b_row = b_all[:, cf:cf + 1], b_t[cf:cf + 1, :]
            b_last = b_col[last:last + 1, :]
            m_prev = m_ref[r:r + 1, 0:1]
            c_prev = c_ref[r]
            n_prev = n_ref[r:r + 1, :]
            q = qkv_ref[:, hd * M_DK:(hd + 1) * M_DK]
            k = qkv_ref[:, M_HEADS * M_DK + hd * M_DK:M_HEADS * M_DK + (hd + 1) * M_DK]
            v = qkv_ref[:, 2 * M_HEADS * M_DK + hd * M_DV:2 * M_HEADS * M_DK + (hd + 1) * M_DV]

            log_d = jnp.where(mask, b_col - b_row + ig_row, -jnp.inf)
            m_inter = b_col + m_prev
            m_row = jnp.maximum(m_inter, jnp.max(log_d, axis=-1, keepdims=True))
            qk = lax.dot_general(q, k, (((1,), (1,)), ((), ())), preferred_element_type=F32)
            s = qk * jnp.exp(log_d - m_row)
            w_inter = jnp.exp(m_inter - m_row)
            qc = jnp.dot(q, c_prev.astype(BF16), preferred_element_type=F32)
            num = jnp.dot(s.astype(BF16), v, preferred_element_type=F32) + w_inter * qc
            qn = jnp.sum(q.astype(F32) * n_prev, axis=-1, keepdims=True)
            den = jnp.sum(s, axis=-1, keepdims=True) + w_inter * qn
            out_ref[:, hd * M_DV:(hd + 1) * M_DV] = num / jnp.maximum(jnp.abs(den), jnp.exp(-m_row))

            log_w = b_last - b_col + ig_col
            m_new = jnp.maximum(b_last + m_prev, jnp.max(log_w, axis=0, keepdims=True))
            w = jnp.exp(log_w - m_new)
            decay = jnp.exp(b_last + m_prev - m_new)
            wv = (w * v.astype(F32)).astype(BF16)
            c_ref[r] = decay * c_prev + lax.dot_general(k, wv, (((0,), (0,)), ((), ())), preferred_element_type=F32)
            n_ref[r:r + 1, :] = decay * n_prev + jnp.sum(w * k.astype(F32), axis=0, keepdims=True)
            m_ref[r:r + 1, :] = jnp.broadcast_to(m_new, (1, LANES))


def _mlstm(qkv, pre, bias, state):
    n = qkv.shape[0]
    nc = n // CHUNK
    width = qkv.shape[1]
    hw = M_HEADS * M_DV
    c0, n0, m0 = state
    fwd = lambda t: (t, 0)
    bwd = lambda t: (nc - 1 - t, 0)
    const2 = lambda t: (0, 0)
    const3 = lambda t: (0, 0, 0)
    st_specs = [pl.BlockSpec(c0.shape, const3), pl.BlockSpec(n0.shape, const2), pl.BlockSpec(m0.shape, const2)]
    blk = 2 * _nbytes((CHUNK, width), BF16) + 2 * _nbytes((CHUNK, LANES), F32) + 2 * _nbytes((CHUNK, hw), F32) \
        + 2 * _nbytes(c0.shape, F32) + 8 * 1024 * 1024
    return pl.pallas_call(
        _mlstm_kernel,
        grid=(nc,),
        in_specs=[pl.BlockSpec((CHUNK, width), fwd), pl.BlockSpec((CHUNK, width), bwd),
                  pl.BlockSpec((CHUNK, LANES), fwd), pl.BlockSpec((CHUNK, LANES), bwd),
                  pl.BlockSpec((1, LANES), const2)] + st_specs,
        out_specs=[pl.BlockSpec((CHUNK, hw), fwd), pl.BlockSpec((CHUNK, hw), bwd)] + st_specs,
        out_shape=[jax.ShapeDtypeStruct((n, hw), F32), jax.ShapeDtypeStruct((n, hw), F32),
                   jax.ShapeDtypeStruct(c0.shape, F32), jax.ShapeDtypeStruct(n0.shape, F32),
                   jax.ShapeDtypeStruct(m0.shape, F32)],
        compiler_params=_params(1, blk), name="mlstm_scan",
    )(qkv, qkv, pre, pre, bias, c0, n0, m0)


def _branch_kernel(gu_ref, hf_ref, hb_ref, osig_ref, sgu_g_ref, sgu_w_ref, sgu_bt_ref, mh_g_ref, o_ref, *, a_width):
    gd = a_width // A_GROUPS
    v = gu_ref[:, a_width:2 * a_width]
    vc = v - jnp.mean(v, axis=-1, keepdims=True)
    vn = (vc * lax.rsqrt(jnp.mean(vc * vc, axis=-1, keepdims=True) + EPS) * sgu_g_ref[...]).astype(BF16)
    for g in range(A_GROUPS):
        sl = slice(g * gd, (g + 1) * gd)
        s = jnp.dot(sgu_w_ref[g], vn[:, sl], preferred_element_type=F32) + sgu_bt_ref[:, g:g + 1]
        o_ref[:, sl] = (gu_ref[:, sl] * s).astype(o_ref.dtype)
    for hd in range(M_HEADS):
        sl = slice(hd * M_DV, (hd + 1) * M_DV)
        h = hf_ref[:, sl] + hb_ref[:, sl]
        hn = h * lax.rsqrt(jnp.mean(h * h, axis=-1, keepdims=True) + EPS)
        o_ref[:, a_width + hd * M_DV:a_width + (hd + 1) * M_DV] = (hn * mh_g_ref[:, sl] * osig_ref[:, sl]).astype(o_ref.dtype)


def _branch_inputs(gu, hf, hb, sig, sgu_g, sgu_w, sgu_bt, mh_g, a_width):
    n = gu.shape[0]
    hw = M_HEADS * M_DV
    const2 = lambda t: (0, 0)
    blk = _nbytes((CHUNK, 2 * a_width), F32) + 3 * _nbytes((CHUNK, hw), F32) + _nbytes((CHUNK, a_width + hw), BF16) \
        + 4 * 1024 * 1024
    return pl.pallas_call(
        functools.partial(_branch_kernel, a_width=a_width),
        grid=(n // CHUNK,),
        in_specs=[pl.BlockSpec((CHUNK, 2 * a_width), lambda t: (t, 0)),
                  pl.BlockSpec((CHUNK, hw), lambda t: (t, 0)), pl.BlockSpec((CHUNK, hw), lambda t: (t, 0)),
                  pl.BlockSpec((CHUNK, hw), lambda t: (t, 0)),
                  pl.BlockSpec((1, a_width), const2), pl.BlockSpec((A_GROUPS, CHUNK, CHUNK), lambda t: (0, 0, 0)),
                  pl.BlockSpec((CHUNK, LANES), const2), pl.BlockSpec((1, hw), const2)],
        out_specs=pl.BlockSpec((CHUNK, a_width + hw), lambda t: (t, 0)),
        out_shape=jax.ShapeDtypeStruct((n, a_width + hw), BF16),
        compiler_params=_params(1, blk), name="branch_inputs",
    )(gu, hf, hb, sig, sgu_g, sgu_w, sgu_bt, mh_g)


def _final_norm_kernel(x_ref, g_ref, o_ref):
    o_ref[...] = _rms(x_ref[...], g_ref[...])


def _final_norm(x, g):
    m, d = x.shape
    tm = _tile(m, 256)
    return pl.pallas_call(
        _final_norm_kernel, grid=(m // tm,),
        in_specs=[pl.BlockSpec((tm, d), lambda i: (i, 0)), pl.BlockSpec((1, d), lambda i: (0, 0))],
        out_specs=pl.BlockSpec((tm, d), lambda i: (i, 0)),
        out_shape=jax.ShapeDtypeStruct((m, d), F32),
        compiler_params=_params(1, 2 * _nbytes((tm, d), F32)), name="final_norm",
    )(x, g)


def kernel(x, c, ctx, c_ctx, ada_w, ada_b, norm_mix, w_in, sgu_norm, sgu_w, sgu_b, gate_bias, mh_norm, w_br_a, w_br_b, w_out, norm_ffn, dense_w13, dense_w2, moe_router, moe_w13, moe_w2, final_norm):
    bsz, seq, d = x.shape
    assert bsz == 1 and seq % CHUNK == 0 and ctx.shape[1] % CHUNK == 0
    depth = w_in.shape[0]
    a_width = sgu_norm.shape[1]
    qk_w = M_HEADS * M_DK
    hw = M_HEADS * M_DV
    assert mh_norm.shape[1] == hw and a_width == hw and w_in.shape[2] == 2 * a_width + 2 * qk_w + 2 * hw + IF_WIDTH + 2 * d
    o_uv, o_q, o_o, o_if, o_g = 0, 2 * a_width, 2 * a_width + 2 * qk_w + hw, 2 * a_width + 2 * qk_w + 2 * hw, \
        2 * a_width + 2 * qk_w + 2 * hw + IF_WIDTH

    cond = jnp.zeros((8, d), F32).at[0].set(c[0]).at[1].set(c_ctx)
    mod = _adaln(cond, ada_w, ada_b)

    q_scale = jnp.concatenate([jnp.full((1, qk_w), M_DK ** -0.5, F32), jnp.ones((1, qk_w + hw), F32)], axis=1)
    zero_state = (jnp.zeros((N_DIRS * M_HEADS, M_DK, M_DV), F32), jnp.zeros((N_DIRS * M_HEADS, M_DK), F32),
                  jnp.zeros((N_DIRS * M_HEADS, LANES), F32))

    def row(a):
        return a.reshape(1, -1)

    xl = x.reshape(seq, d)
    xc = ctx.reshape(ctx.shape[1], d)
    for l in range(depth):
        last = l == depth - 1
        wl = w_in[l]
        w_uv = wl[:, o_uv:o_q].astype(BF16)
        w_qkv = wl[:, o_q:o_o].astype(BF16)
        w_sig = jnp.concatenate([wl[:, o_o:o_if], wl[:, o_g:]], axis=1).astype(BF16)
        w_if = jnp.pad(wl[:, o_if:o_g], ((0, 0), (0, LANES - IF_WIDTH))).astype(BF16)
        bias_if = jnp.pad(gate_bias[l], (0, LANES - IF_WIDTH)).reshape(1, LANES)
        sgu_wb = sgu_w[l].astype(BF16)
        sgu_bt = jnp.pad(sgu_b[l].T, ((0, 0), (0, LANES - A_GROUPS)))
        wa, wb, wo = w_br_a[l].astype(BF16), w_br_b[l].astype(BF16), w_out[l].astype(BF16)
        if l % 2 == 0:
            w13 = dense_w13[l // 2].astype(BF16)
            w2 = dense_w2[l // 2].astype(BF16)
            router = None
        else:
            w13 = moe_w13[l // 2].astype(BF16)
            w2 = moe_w2[l // 2].reshape(-1, d).astype(BF16)
            router = jnp.pad(moe_router[l // 2], ((0, 0), (0, LANES - N_EXPERTS))).astype(BF16)

        def mods(r):
            return [row(mod[l, r, i * d:(i + 1) * d]) for i in range(N_MOD)]

        def project(xs, r):
            sh_m, sc_m = mods(r)[:2]
            h = _norm_mod(xs, row(norm_mix[l]), sh_m, sc_m)
            gu = _mm_act(h, w_uv, "gelu", F32)
            qkv = _mm_colscale(h, w_qkv, q_scale, BF16)
            sig = _mm_act(h, w_sig, "sigmoid", F32)
            pre = _mm_act(h, w_if, "none", F32)
            return gu, qkv, sig, pre

        def mix_and_ffn(xs, r, gu, sig, hf, hb):
            _, _, g_m, sh_f, sc_f, g_f = mods(r)
            t = _branch_inputs(gu, hf, hb, sig, row(sgu_norm[l]), sgu_wb, sgu_bt, row(mh_norm[l]), a_width)
            z = _mm_merge(t, wa, wb, sig)
            xs = _mm_residual(z, wo, xs, g_m)
            if router is None:
                h2 = _norm_mod(xs, row(norm_ffn[l]), sh_f, sc_f)
                act = _ffn_up_dense(h2, w13)
            else:
                h2, gates = _norm_mod(xs, row(norm_ffn[l]), sh_f, sc_f, router=router)
                act = _ffn_up_moe(h2, w13, gates)
            return _mm_residual(act, w2, xs, g_f)

        gu_c, qkv_c, sig_c, pre_c = project(xc, 1)
        gu_l, qkv_l, sig_l, pre_l = project(xl, 0)
        hf_c, hb_c, *ctx_state = _mlstm(qkv_c, pre_c, bias_if, zero_state)
        hf_l, hb_l, *_ = _mlstm(qkv_l, pre_l, bias_if, tuple(ctx_state))
        xl = mix_and_ffn(xl, 0, gu_l, sig_l, hf_l, hb_l)
        if not last:
            xc = mix_and_ffn(xc, 1, gu_c, sig_c, hf_c, hb_c)
    return _final_norm(xl, row(final_norm)).reshape(bsz, seq, d)
```

```python
import functools

import jax
import jax.numpy as jnp
from jax import lax
from jax.experimental import pallas as pl
from jax.experimental.pallas import tpu as pltpu

F32 = jnp.float32
BF16 = jnp.bfloat16

CHUNK = 128
A_GROUPS = 4
M_HEADS = 4
M_DK = 128
M_DV = 256
N_DIRS = 2
IF_WIDTH = N_DIRS * 2 * M_HEADS
GATE_SOFTCAP = 15.0
N_EXPERTS = 8
N_MOD = 6
EPS = 1e-6
MOD_ROWS = 8

LANES = 128
V7X_VMEM_BYTES = 64 * 1024 * 1024
VMEM_LIMIT_CAP = V7X_VMEM_BYTES - 6 * 1024 * 1024
MIB = 1024 * 1024


def _params(n_grid_dims, block_bytes, temp_bytes=0):
    need = 2 * block_bytes + temp_bytes + 8 * MIB
    return pltpu.CompilerParams(
        dimension_semantics=("arbitrary",) * n_grid_dims,
        vmem_limit_bytes=int(min(max(need, 16 * MIB), VMEM_LIMIT_CAP)))


def _nbytes(shape, dtype):
    n = 1
    for s in shape:
        n *= s
    return n * jnp.dtype(dtype).itemsize


def _tile(n, pref, unit=LANES):
    if n <= pref:
        return n
    t = pref - pref % unit
    while n % t:
        t -= unit
    assert t > 0
    return t


def _sigmoid(x):
    return 1.0 / (1.0 + jnp.exp(-x))


def _gelu(x):
    return 0.5 * x * (1.0 + jnp.tanh(0.7978845608028654 * (x + 0.044715 * (x * x * x))))


def _log_sigmoid(x):
    return jnp.minimum(x, 0.0) - jnp.log1p(jnp.exp(-jnp.abs(x)))


def _row_select(mod_ref, tile_rows, tile_index, n_ctx):
    rows = tile_index * tile_rows + lax.broadcasted_iota(jnp.int32, (tile_rows, 1), 0)
    return jnp.where(rows < n_ctx, mod_ref[1:2, :], mod_ref[0:1, :])


def _cast_kernel(w_ref, o_ref):
    o_ref[...] = w_ref[...].astype(o_ref.dtype)


def _cast_bf16(w):
    n_l, r, c = w.shape
    tr = _tile(r, max(8, (8 * MIB) // (4 * c)), unit=16)
    return pl.pallas_call(
        _cast_kernel, grid=(n_l, r // tr),
        in_specs=[pl.BlockSpec((None, tr, c), lambda l, i: (l, i, 0))],
        out_specs=pl.BlockSpec((None, tr, c), lambda l, i: (l, i, 0)),
        out_shape=jax.ShapeDtypeStruct(w.shape, BF16),
        compiler_params=_params(2, _nbytes((tr, c), F32) + _nbytes((tr, c), BF16)), name="cast_bf16",
    )(w)


def _cast_w_in_kernel(w_ref, o_ref, oif_ref, *, n_main, n_gate):
    o_ref[:, :n_main] = w_ref[:, :n_main].astype(BF16)
    o_ref[:, n_main:] = w_ref[:, n_main + IF_WIDTH:n_main + IF_WIDTH + n_gate].astype(BF16)
    win = w_ref[:, n_main:n_main + LANES]
    lane = lax.broadcasted_iota(jnp.int32, win.shape, 1)
    oif_ref[...] = jnp.where(lane < IF_WIDTH, win, 0.0).astype(BF16)


def _cast_w_in(w_in, n_main, n_gate):
    n_l, d, n = w_in.shape
    tr = _tile(d, 128, unit=16)
    return pl.pallas_call(
        functools.partial(_cast_w_in_kernel, n_main=n_main, n_gate=n_gate), grid=(n_l, d // tr),
        in_specs=[pl.BlockSpec((None, tr, n), lambda l, i: (l, i, 0))],
        out_specs=[pl.BlockSpec((None, tr, n_main + n_gate), lambda l, i: (l, i, 0)),
                   pl.BlockSpec((None, tr, LANES), lambda l, i: (l, i, 0))],
        out_shape=[jax.ShapeDtypeStruct((n_l, d, n_main + n_gate), BF16), jax.ShapeDtypeStruct((n_l, d, LANES), BF16)],
        compiler_params=_params(2, _nbytes((tr, n), F32) * 2), name="cast_w_in",
    )(w_in)


def _adaln_kernel(cond_ref, w_ref, b_ref, o_ref):
    c = cond_ref[...]
    s = (c * _sigmoid(c)).astype(BF16)
    acc = jnp.dot(s, w_ref[...].astype(BF16), preferred_element_type=F32)
    o_ref[...] = acc + b_ref[...]


def _adaln(cond, ada_w, ada_b):
    n_layers, d, n = ada_w.shape
    r = cond.shape[0]
    tn = _tile(n, 512)
    blk = _nbytes((d, tn), F32) + _nbytes((r, d), F32) + 2 * _nbytes((r, tn), F32)
    return pl.pallas_call(
        _adaln_kernel,
        grid=(n_layers, n // tn),
        in_specs=[pl.BlockSpec((r, d), lambda l, j: (0, 0)),
                  pl.BlockSpec((None, d, tn), lambda l, j: (l, 0, j)),
                  pl.BlockSpec((None, 1, tn), lambda l, j: (l, 0, j))],
        out_specs=pl.BlockSpec((None, r, tn), lambda l, j: (l, 0, j)),
        out_shape=jax.ShapeDtypeStruct((n_layers, r, n), F32),
        compiler_params=_params(2, blk, _nbytes((d, tn), BF16)), name="adaln",
    )(cond, ada_w, ada_b.reshape(n_layers, 1, n))


ROW_GROUP = 16
COL_CHUNK = 1024


def _inv_rms(x_ref, r0, rows):
    d = x_ref.shape[1]
    cc = min(COL_CHUNK, d)
    ss = None
    for c0 in range(0, d, cc):
        xc = x_ref[r0:r0 + rows, c0:c0 + cc]
        part = jnp.sum(xc * xc, axis=-1, keepdims=True)
        ss = part if ss is None else ss + part
    return lax.rsqrt(ss / d + EPS)


def _norm_mod_kernel(x_ref, g_ref, shift_ref, scale_ref, *rest, n_ctx, routed):
    tm, d = x_ref.shape
    cc = min(COL_CHUNK, d)
    is_ctx = pl.program_id(0) * tm < n_ctx
    shift = jnp.where(is_ctx, shift_ref[1:2, :], shift_ref[0:1, :])
    onep = 1.0 + jnp.where(is_ctx, scale_ref[1:2, :], scale_ref[0:1, :])
    o_ref = rest[1] if routed else rest[0]
    for r0 in range(0, tm, ROW_GROUP):
        inv = _inv_rms(x_ref, r0, ROW_GROUP)
        for c0 in range(0, d, cc):
            cs = slice(c0, c0 + cc)
            y = x_ref[r0:r0 + ROW_GROUP, cs] * inv * g_ref[:, cs]
            o_ref[r0:r0 + ROW_GROUP, cs] = (y * onep[:, cs] + shift[:, cs]).astype(BF16)
    if not routed:
        return
    r_ref, _, gate_ref = rest
    logits = jnp.dot(o_ref[...], r_ref[...], preferred_element_type=F32)
    lane = lax.broadcasted_iota(jnp.int32, logits.shape, 1)
    logits = jnp.where(lane < N_EXPERTS, logits, -jnp.inf)
    m1 = jnp.max(logits, axis=-1, keepdims=True)
    i1 = jnp.min(jnp.where(logits == m1, lane, LANES), axis=-1, keepdims=True)
    rest_l = jnp.where(lane == i1, -jnp.inf, logits)
    m2 = jnp.max(rest_l, axis=-1, keepdims=True)
    i2 = jnp.min(jnp.where(rest_l == m2, lane, LANES), axis=-1, keepdims=True)
    e2 = jnp.exp(m2 - m1)
    gate_ref[...] = jnp.where(lane == i1, 1.0 / (1.0 + e2), 0.0) + jnp.where(lane == i2, e2 / (1.0 + e2), 0.0)


def _norm_mod(x, g, mod, l, k_shift, n_ctx, router=None):
    m, d = x.shape
    tm = _tile(n_ctx, 256, unit=ROW_GROUP)
    assert n_ctx % tm == 0 and m % tm == 0
    in_specs = [pl.BlockSpec((tm, d), lambda i: (i, 0)),
                pl.BlockSpec((None, 1, d), lambda i: (l, 0, 0)),
                pl.BlockSpec((None, MOD_ROWS, d), lambda i: (l, 0, k_shift)),
                pl.BlockSpec((None, MOD_ROWS, d), lambda i: (l, 0, k_shift + 1))]
    blk = _nbytes((tm, d), F32) + _nbytes((tm, d), BF16)
    tmp = 3 * _nbytes((tm, d), F32)
    kern = functools.partial(_norm_mod_kernel, n_ctx=n_ctx, routed=router is not None)
    if router is None:
        return pl.pallas_call(
            kern, grid=(m // tm,), in_specs=in_specs,
            out_specs=pl.BlockSpec((tm, d), lambda i: (i, 0)),
            out_shape=jax.ShapeDtypeStruct((m, d), BF16),
            compiler_params=_params(1, blk, tmp), name="norm_mod",
        )(x, g, mod, mod)
    return pl.pallas_call(
        kern, grid=(m // tm,),
        in_specs=in_specs + [pl.BlockSpec((d, LANES), lambda i: (0, 0))],
        out_specs=[pl.BlockSpec((tm, d), lambda i: (i, 0)), pl.BlockSpec((tm, LANES), lambda i: (i, 0))],
        out_shape=[jax.ShapeDtypeStruct((m, d), BF16), jax.ShapeDtypeStruct((m, LANES), F32)],
        compiler_params=_params(1, blk + _nbytes((d, LANES), BF16), tmp), name="norm_mod_router",
    )(x, g, mod, mod, router)


def _mm_act_kernel(x_ref, w_ref, *rest, act):
    acc = jnp.dot(x_ref[...], w_ref[...], preferred_element_type=F32)
    if act == "gelu":
        acc = _gelu(acc)
    elif act == "sigmoid":
        acc = _sigmoid(acc)
    elif act == "colscale":
        acc = acc * rest[0][...]
    rest[-1][...] = acc.astype(rest[-1].dtype)


def _mm_act(x, w, l, col0, n, act, out_dtype, col_scale=None, tm_pref=1280, tn_pref=1024):
    m, k = x.shape
    tm, tn = _tile(m, tm_pref), _tile(n, tn_pref)
    assert col0 % tn == 0
    cb = col0 // tn
    in_specs = [pl.BlockSpec((tm, k), lambda i, j: (i, 0)), pl.BlockSpec((None, k, tn), lambda i, j: (l, 0, cb + j))]
    args = [x, w]
    if act == "colscale":
        in_specs.append(pl.BlockSpec((1, tn), lambda i, j: (0, j)))
        args.append(col_scale)
    blk = _nbytes((tm, k), BF16) + _nbytes((k, tn), BF16) + _nbytes((tm, tn), out_dtype)
    return pl.pallas_call(
        functools.partial(_mm_act_kernel, act=act),
        grid=(m // tm, n // tn),
        in_specs=in_specs,
        out_specs=pl.BlockSpec((tm, tn), lambda i, j: (i, j)),
        out_shape=jax.ShapeDtypeStruct((m, n), out_dtype),
        compiler_params=_params(2, blk, 2 * _nbytes((tm, tn), F32)), name="mm_" + act,
    )(*args)


def _mm_residual_kernel(z_ref, w_ref, x_ref, g_ref, o_ref, *, n_ctx):
    acc = jnp.dot(z_ref[...], w_ref[...], preferred_element_type=F32)
    gate = _row_select(g_ref, z_ref.shape[0], pl.program_id(0), n_ctx)
    o_ref[...] = x_ref[...] + gate * acc


def _mm_residual(z, w, l, x, mod, l_mod, k_gate, n_ctx, tm_pref=1280, tn_pref=512):
    m, k = z.shape
    n = w.shape[2]
    tm, tn = _tile(m, tm_pref), _tile(n, tn_pref)
    gb = k_gate * (n // tn)
    blk = _nbytes((tm, k), BF16) + _nbytes((k, tn), BF16) + 2 * _nbytes((tm, tn), F32)
    return pl.pallas_call(
        functools.partial(_mm_residual_kernel, n_ctx=n_ctx),
        grid=(m // tm, n // tn),
        in_specs=[pl.BlockSpec((tm, k), lambda i, j: (i, 0)), pl.BlockSpec((None, k, tn), lambda i, j: (l, 0, j)),
                  pl.BlockSpec((tm, tn), lambda i, j: (i, j)),
                  pl.BlockSpec((None, MOD_ROWS, tn), lambda i, j: (l_mod, 0, gb + j))],
        out_specs=pl.BlockSpec((tm, tn), lambda i, j: (i, j)),
        out_shape=jax.ShapeDtypeStruct((m, n), F32),
        compiler_params=_params(2, blk, 2 * _nbytes((tm, tn), F32)), name="mm_residual",
    )(z, w, x, mod)


def _mm_merge_kernel(ta_ref, tb_ref, wa_ref, wb_ref, ga_ref, gb_ref, o_ref):
    ya = jnp.dot(ta_ref[...], wa_ref[...], preferred_element_type=F32)
    yb = jnp.dot(tb_ref[...], wb_ref[...], preferred_element_type=F32)
    o_ref[...] = (ga_ref[...].astype(F32) * ya + gb_ref[...].astype(F32) * yb).astype(o_ref.dtype)


def _mm_merge(t, wa, wb, l, sig, tm_pref=1280, tn_pref=1024):
    m = t.shape[0]
    _, ka, n = wa.shape
    assert wb.shape == wa.shape and t.shape[1] == 2 * ka
    ko = sig.shape[1] - 2 * n
    tm, tn = _tile(m, tm_pref), _tile(n, tn_pref)
    off_a = ko // tn
    off_b = (ko + n) // tn
    assert off_a * tn == ko and off_b * tn == ko + n
    blk = 2 * _nbytes((tm, ka), BF16) + 2 * _nbytes((ka, tn), BF16) + 2 * _nbytes((tm, tn), sig.dtype) \
        + _nbytes((tm, tn), BF16)
    return pl.pallas_call(
        _mm_merge_kernel,
        grid=(m // tm, n // tn),
        in_specs=[pl.BlockSpec((tm, ka), lambda i, j: (i, 0)), pl.BlockSpec((tm, ka), lambda i, j: (i, 1)),
                  pl.BlockSpec((None, ka, tn), lambda i, j: (l, 0, j)), pl.BlockSpec((None, ka, tn), lambda i, j: (l, 0, j)),
                  pl.BlockSpec((tm, tn), lambda i, j: (i, j + off_a)),
                  pl.BlockSpec((tm, tn), lambda i, j: (i, j + off_b))],
        out_specs=pl.BlockSpec((tm, tn), lambda i, j: (i, j)),
        out_shape=jax.ShapeDtypeStruct((m, n), BF16),
        compiler_params=_params(2, blk, 3 * _nbytes((tm, tn), F32)), name="mm_merge",
    )(t, t, wa, wb, sig, sig)


def _ffn_up_kernel(h_ref, wg_ref, wu_ref, *rest, routed):
    h = h_ref[...]
    g = jnp.dot(h, wg_ref[...], preferred_element_type=F32)
    u = jnp.dot(h, wu_ref[...], preferred_element_type=F32)
    act = g * _sigmoid(g) * u
    if routed:
        gates = rest[0][...]
        lane = lax.broadcasted_iota(jnp.int32, gates.shape, 1)
        act = act * jnp.sum(jnp.where(lane == pl.program_id(1), gates, 0.0), axis=-1, keepdims=True)
    rest[-1][...] = act.astype(rest[-1].dtype)


def _ffn_up_dense(h, w13, l, tm_pref=1280, tn_pref=512):
    m, k = h.shape
    f = w13.shape[2] // 2
    tm, tn = _tile(m, tm_pref), _tile(f, tn_pref)
    nb = f // tn
    blk = _nbytes((tm, k), BF16) + 2 * _nbytes((k, tn), BF16) + _nbytes((tm, tn), BF16)
    return pl.pallas_call(
        functools.partial(_ffn_up_kernel, routed=False),
        grid=(m // tm, nb),
        in_specs=[pl.BlockSpec((tm, k), lambda i, j: (i, 0)), pl.BlockSpec((None, k, tn), lambda i, j: (l, 0, j)),
                  pl.BlockSpec((None, k, tn), lambda i, j: (l, 0, j + nb))],
        out_specs=pl.BlockSpec((tm, tn), lambda i, j: (i, j)),
        out_shape=jax.ShapeDtypeStruct((m, f), BF16),
        compiler_params=_params(2, blk, 3 * _nbytes((tm, tn), F32)), name="ffn_up_dense",
    )(h, w13, w13)


def _ffn_up_moe(h, w13, l, gates, tm_pref=1280):
    m, k = h.shape
    two_de = w13.shape[2]
    de = two_de // 2
    tm = _tile(m, tm_pref)
    e0 = l * N_EXPERTS
    blk = _nbytes((tm, k), BF16) + 2 * _nbytes((k, de), BF16) + _nbytes((tm, de), BF16) + _nbytes((tm, LANES), F32)
    return pl.pallas_call(
        functools.partial(_ffn_up_kernel, routed=True),
        grid=(m // tm, N_EXPERTS),
        in_specs=[pl.BlockSpec((tm, k), lambda i, j: (i, 0)), pl.BlockSpec((None, k, de), lambda i, j: (e0 + j, 0, 0)),
                  pl.BlockSpec((None, k, de), lambda i, j: (e0 + j, 0, 1)), pl.BlockSpec((tm, LANES), lambda i, j: (i, 0))],
        out_specs=pl.BlockSpec((tm, de), lambda i, j: (i, j)),
        out_shape=jax.ShapeDtypeStruct((m, N_EXPERTS * de), BF16),
        compiler_params=_params(2, blk, 3 * _nbytes((tm, de), F32)), name="ffn_up_moe",
    )(h, w13, w13, gates)


def _split3(x):
    hi = x.astype(BF16)
    r1 = x - hi.astype(F32)
    mid = r1.astype(BF16)
    lo = (r1 - mid.astype(F32)).astype(BF16)
    return hi, mid, lo


def _mlstm_kernel(qkv_f_ref, qkv_b_ref, pre_f_ref, pre_b_ref, bias_ref, hf_ref, hb_ref, c_ref, n_ref, m_ref):
    @pl.when(pl.program_id(0) == 0)
    def _():
        c_ref[...] = jnp.zeros_like(c_ref)
        n_ref[...] = jnp.zeros_like(n_ref)
        m_ref[...] = jnp.zeros_like(m_ref)

    rows = lax.broadcasted_iota(jnp.int32, (CHUNK, CHUNK), 0)
    cols = lax.broadcasted_iota(jnp.int32, (CHUNK, CHUNK), 1)
    for d, (qkv_ref, pre_ref, out_ref) in enumerate(((qkv_f_ref, pre_f_ref, hf_ref), (qkv_b_ref, pre_b_ref, hb_ref))):
        mask = (rows >= cols) if d == 0 else (rows <= cols)
        tri = jnp.where(mask, 1.0, 0.0).astype(BF16)
        a = pre_ref[...] + bias_ref[...]
        a = GATE_SOFTCAP * jnp.tanh(a / GATE_SOFTCAP)
        lf = _log_sigmoid(a)
        hi, mid, lo = _split3(lf)
        b_all = (jnp.dot(tri, hi, preferred_element_type=F32) + jnp.dot(tri, mid, preferred_element_type=F32)
                 + jnp.dot(tri, lo, preferred_element_type=F32))
        a_t = a.T
        b_t = b_all.T
        last = CHUNK - 1 if d == 0 else 0
        for hd in range(M_HEADS):
            ci = d * 2 * M_HEADS + hd
            cf = ci + M_HEADS
            r = d * M_HEADS + hd
            ig_col, ig_row = a[:, ci:ci + 1], a_t[ci:ci + 1, :]
            b_col, b_row = b_all[:, cf:cf + 1], b_t[cf:cf + 1, :]
            b_last = b_col[last:last + 1, :]
            m_prev = m_ref[r:r + 1, 0:1]
            c_prev = c_ref[r]
            n_prev = n_ref[r:r + 1, :]
            q = qkv_ref[:, hd * M_DK:(hd + 1) * M_DK]
            k = qkv_ref[:, M_HEADS * M_DK + hd * M_DK:M_HEADS * M_DK + (hd + 1) * M_DK]
            v = qkv_ref[:, 2 * M_HEADS * M_DK + hd * M_DV:2 * M_HEADS * M_DK + (hd + 1) * M_DV]

            log_d = jnp.where(mask, b_col - b_row + ig_row, -jnp.inf)
            m_inter = b_col + m_prev
            m_row = jnp.maximum(m_inter, jnp.max(log_d, axis=-1, keepdims=True))
            qk = lax.dot_general(q, k, (((1,), (1,)), ((), ())), preferred_element_type=F32)
            s = qk * jnp.exp(log_d - m_row)
            w_inter = jnp.exp(m_inter - m_row)
            qc = jnp.dot(q, c_prev.astype(BF16), preferred_element_type=F32)
            num = jnp.dot(s.astype(BF16), v, preferred_element_type=F32) + w_inter * qc
            qn = jnp.sum(q.astype(F32) * n_prev, axis=-1, keepdims=True)
            den = jnp.sum(s, axis=-1, keepdims=True) + w_inter * qn
            out_ref[:, hd * M_DV:(hd + 1) * M_DV] = num / jnp.maximum(jnp.abs(den), jnp.exp(-m_row))

            log_w = b_last - b_col + ig_col
            m_new = jnp.maximum(b_last + m_prev, jnp.max(log_w, axis=0, keepdims=True))
            w = jnp.exp(log_w - m_new)
            decay = jnp.exp(b_last + m_prev - m_new)
            wv = (w * v.astype(F32)).astype(BF16)
            c_ref[r] = decay * c_prev + lax.dot_general(k, wv, (((0,), (0,)), ((), ())), preferred_element_type=F32)
            n_ref[r:r + 1, :] = decay * n_prev + jnp.sum(w * k.astype(F32), axis=0, keepdims=True)
            m_ref[r:r + 1, :] = jnp.broadcast_to(m_new, (1, LANES))


def _mlstm(qkv, pre, bias, l, n_ctx):
    n = qkv.shape[0]
    nc = n // CHUNK
    ncc = n_ctx // CHUNK
    width = qkv.shape[1]
    hw = M_HEADS * M_DV
    n_chain = N_DIRS * M_HEADS
    fwd = lambda t: (t, 0)
    bwd = lambda t: (jnp.where(t < ncc, ncc - 1 - t, ncc + nc - 1 - t), 0)
    blk = 2 * _nbytes((CHUNK, width), BF16) + 2 * _nbytes((CHUNK, LANES), F32) + 2 * _nbytes((CHUNK, hw), F32)
    return pl.pallas_call(
        _mlstm_kernel,
        grid=(nc,),
        in_specs=[pl.BlockSpec((CHUNK, width), fwd), pl.BlockSpec((CHUNK, width), bwd),
                  pl.BlockSpec((CHUNK, LANES), fwd), pl.BlockSpec((CHUNK, LANES), bwd),
                  pl.BlockSpec((None, 1, LANES), lambda t: (l, 0, 0))],
        out_specs=[pl.BlockSpec((CHUNK, hw), fwd), pl.BlockSpec((CHUNK, hw), bwd)],
        out_shape=[jax.ShapeDtypeStruct((n, hw), F32), jax.ShapeDtypeStruct((n, hw), F32)],
        scratch_shapes=[pltpu.VMEM((n_chain, M_DK, M_DV), F32), pltpu.VMEM((n_chain, M_DK), F32),
                        pltpu.VMEM((n_chain, LANES), F32)],
        compiler_params=_params(1, blk, 10 * MIB), name="mlstm_scan",
    )(qkv, qkv, pre, pre, bias)


def _branch_kernel(gu_ref, hf_ref, hb_ref, osig_ref, sgu_g_ref, sgu_w_ref, sgu_bt_ref, mh_g_ref, o_ref, *, a_width):
    gd = a_width // A_GROUPS
    for r0 in range(0, gu_ref.shape[0], CHUNK):
        rs = slice(r0, r0 + CHUNK)
        v = gu_ref[rs, a_width:2 * a_width].astype(F32)
        vc = v - jnp.mean(v, axis=-1, keepdims=True)
        vn = (vc * lax.rsqrt(jnp.mean(vc * vc, axis=-1, keepdims=True) + EPS) * sgu_g_ref[...]).astype(BF16)
        for g in range(A_GROUPS):
            sl = slice(g * gd, (g + 1) * gd)
            s = jnp.dot(sgu_w_ref[g], vn[:, sl], preferred_element_type=F32) + sgu_bt_ref[:, g:g + 1]
            o_ref[rs, sl] = (gu_ref[rs, sl].astype(F32) * s).astype(o_ref.dtype)
        for hd in range(M_HEADS):
            sl = slice(hd * M_DV, (hd + 1) * M_DV)
            h = hf_ref[rs, sl] + hb_ref[rs, sl]
            hn = h * lax.rsqrt(jnp.mean(h * h, axis=-1, keepdims=True) + EPS)
            o_ref[rs, a_width + hd * M_DV:a_width + (hd + 1) * M_DV] = \
                (hn * mh_g_ref[:, sl] * osig_ref[rs, sl].astype(F32)).astype(o_ref.dtype)


def _branch_inputs(gu, hf, hb, sig, sgu_g, sgu_w, sgu_bt, mh_g, l, a_width):
    n = gu.shape[0]
    hw = M_HEADS * M_DV
    tr = 2 * CHUNK if n % (2 * CHUNK) == 0 else CHUNK
    blk = _nbytes((tr, 2 * a_width), gu.dtype) + 2 * _nbytes((tr, hw), F32) + _nbytes((tr, hw), sig.dtype) \
        + _nbytes((tr, a_width + hw), BF16)
    return pl.pallas_call(
        functools.partial(_branch_kernel, a_width=a_width),
        grid=(n // tr,),
        in_specs=[pl.BlockSpec((tr, 2 * a_width), lambda t: (t, 0)),
                  pl.BlockSpec((tr, hw), lambda t: (t, 0)), pl.BlockSpec((tr, hw), lambda t: (t, 0)),
                  pl.BlockSpec((tr, hw), lambda t: (t, 0)),
                  pl.BlockSpec((None, 1, a_width), lambda t: (l, 0, 0)),
                  pl.BlockSpec((None, A_GROUPS, CHUNK, CHUNK), lambda t: (l, 0, 0, 0)),
                  pl.BlockSpec((None, CHUNK, LANES), lambda t: (l, 0, 0)),
                  pl.BlockSpec((None, 1, hw), lambda t: (l, 0, 0))],
        out_specs=pl.BlockSpec((tr, a_width + hw), lambda t: (t, 0)),
        out_shape=jax.ShapeDtypeStruct((n, a_width + hw), BF16),
        compiler_params=_params(1, blk, 6 * MIB), name="branch_inputs",
    )(gu, hf, hb, sig, sgu_g, sgu_w, sgu_bt, mh_g)


def _final_norm_kernel(x_ref, g_ref, o_ref):
    tm, d = x_ref.shape
    cc = min(COL_CHUNK, d)
    for r0 in range(0, tm, ROW_GROUP):
        inv = _inv_rms(x_ref, r0, ROW_GROUP)
        for c0 in range(0, d, cc):
            cs = slice(c0, c0 + cc)
            o_ref[r0:r0 + ROW_GROUP, cs] = x_ref[r0:r0 + ROW_GROUP, cs] * inv * g_ref[:, cs]


def _final_norm(x, g, n_ctx):
    m, d = x.shape
    tm = _tile(n_ctx, 256)
    skip = n_ctx // tm
    assert skip * tm == n_ctx and (m - n_ctx) % tm == 0
    return pl.pallas_call(
        _final_norm_kernel, grid=((m - n_ctx) // tm,),
        in_specs=[pl.BlockSpec((tm, d), lambda i: (i + skip, 0)), pl.BlockSpec((1, d), lambda i: (0, 0))],
        out_specs=pl.BlockSpec((tm, d), lambda i: (i, 0)),
        out_shape=jax.ShapeDtypeStruct((m - n_ctx, d), F32),
        compiler_params=_params(1, 2 * _nbytes((tm, d), F32), 2 * _nbytes((tm, d), F32)), name="final_norm",
    )(x, g)


def kernel(x, c, ctx, c_ctx, ada_w, ada_b, norm_mix, w_in, sgu_norm, sgu_w, sgu_b, gate_bias, mh_norm, w_br_a, w_br_b, w_out, norm_ffn, dense_w13, dense_w2, moe_router, moe_w13, moe_w2, final_norm):
    bsz, seq, d = x.shape
    n_ctx = ctx.shape[1]
    assert bsz == 1 and seq % CHUNK == 0 and n_ctx % CHUNK == 0
    depth = w_in.shape[0]
    a_width = sgu_norm.shape[1]
    qk_w = M_HEADS * M_DK
    hw = M_HEADS * M_DV
    n_main = 2 * a_width + 2 * qk_w + 2 * hw
    n_gate = 2 * d
    assert mh_norm.shape[1] == hw and a_width == hw and w_in.shape[2] == n_main + IF_WIDTH + n_gate
    o_q, o_o = 2 * a_width, 2 * a_width + 2 * qk_w + hw

    w_main, w_if = _cast_w_in(w_in, n_main, n_gate)
    wa, wb, wo = _cast_bf16(w_br_a), _cast_bf16(w_br_b), _cast_bf16(w_out)
    w13_d, w2_d = _cast_bf16(dense_w13), _cast_bf16(dense_w2)
    w13_m = _cast_bf16(moe_w13.reshape((-1,) + moe_w13.shape[2:]))
    w2_m = _cast_bf16(moe_w2.reshape(moe_w2.shape[0], -1, d))
    sgu_wb = _cast_bf16(sgu_w.reshape(depth, A_GROUPS * CHUNK, CHUNK)).reshape(depth, A_GROUPS, CHUNK, CHUNK)
    router = jnp.pad(moe_router, ((0, 0), (0, 0), (0, LANES - N_EXPERTS))).astype(BF16)

    cond = jnp.zeros((MOD_ROWS, d), F32).at[0].set(c[0]).at[1].set(c_ctx)
    mod = _adaln(cond, ada_w, ada_b)

    q_scale = jnp.concatenate([jnp.full((1, qk_w), M_DK ** -0.5, F32), jnp.ones((1, qk_w + hw), F32)], axis=1)
    bias_if = jnp.pad(gate_bias, ((0, 0), (0, LANES - IF_WIDTH))).reshape(depth, 1, LANES)
    sgu_bt = jnp.pad(jnp.swapaxes(sgu_b, 1, 2), ((0, 0), (0, 0), (0, LANES - A_GROUPS)))
    g_mix, g_ffn = norm_mix.reshape(depth, 1, d), norm_ffn.reshape(depth, 1, d)
    g_sgu, g_mh = sgu_norm.reshape(depth, 1, a_width), mh_norm.reshape(depth, 1, hw)

    xs = jnp.concatenate([ctx.reshape(n_ctx, d), x.reshape(seq, d)], axis=0)
    for l in range(depth):
        h = _norm_mod(xs, g_mix, mod, l, 0, n_ctx)
        gu = _mm_act(h, w_main, l, 0, o_q, "gelu", BF16)
        qkv = _mm_act(h, w_main, l, o_q, o_o - o_q, "colscale", BF16, col_scale=q_scale)
        sig = _mm_act(h, w_main, l, o_o, hw + n_gate, "sigmoid", BF16)
        pre = _mm_act(h, w_if, l, 0, LANES, "none", F32)
        hf, hb = _mlstm(qkv, pre, bias_if, l, n_ctx)
        t = _branch_inputs(gu, hf, hb, sig, g_sgu, sgu_wb, sgu_bt, g_mh, l, a_width)
        z = _mm_merge(t, wa, wb, l, sig)
        xs = _mm_residual(z, wo, l, xs, mod, l, 2, n_ctx)
        if l % 2 == 0:
            h2 = _norm_mod(xs, g_ffn, mod, l, 3, n_ctx)
            act = _ffn_up_dense(h2, w13_d, l // 2)
            xs = _mm_residual(act, w2_d, l // 2, xs, mod, l, 5, n_ctx)
        else:
            h2, gates = _norm_mod(xs, g_ffn, mod, l, 3, n_ctx, router=router[l // 2])
            act = _ffn_up_moe(h2, w13_m, l // 2, gates)
            xs = _mm_residual(act, w2_m, l // 2, xs, mod, l, 5, n_ctx)
    return _final_norm(xs, final_norm.reshape(1, d), n_ctx).reshape(bsz, seq, d)
```

```python
import functools

import jax
import jax.numpy as jnp
from jax import lax
from jax.experimental import pallas as pl
from jax.experimental.pallas import tpu as pltpu

F32 = jnp.float32
BF16 = jnp.bfloat16

CHUNK = 128
A_GROUPS = 4
M_HEADS = 4
M_DK = 128
M_DV = 256
N_DIRS = 2
IF_WIDTH = N_DIRS * 2 * M_HEADS
GATE_SOFTCAP = 15.0
N_EXPERTS = 8
N_MOD = 6
EPS = 1e-6
MOD_ROWS = 8

LANES = 128
V7X_VMEM_BYTES = 64 * 1024 * 1024
VMEM_LIMIT_CAP = V7X_VMEM_BYTES - 6 * 1024 * 1024
MIB = 1024 * 1024


def _params(n_grid_dims, block_bytes, temp_bytes=0):
    need = 2 * block_bytes + temp_bytes + 8 * MIB
    return pltpu.CompilerParams(
        dimension_semantics=("arbitrary",) * n_grid_dims,
        vmem_limit_bytes=int(min(max(need, 16 * MIB), VMEM_LIMIT_CAP)))


def _nbytes(shape, dtype):
    n = 1
    for s in shape:
        n *= s
    return n * jnp.dtype(dtype).itemsize


def _tile(n, pref, unit=LANES):
    if n <= pref:
        return n
    t = pref - pref % unit
    while n % t:
        t -= unit
    assert t > 0
    return t


def _sigmoid(x):
    return 1.0 / (1.0 + jnp.exp(-x))


def _gelu(x):
    return 0.5 * x * (1.0 + jnp.tanh(0.7978845608028654 * (x + 0.044715 * (x * x * x))))


def _log_sigmoid(x):
    return jnp.minimum(x, 0.0) - jnp.log1p(jnp.exp(-jnp.abs(x)))


def _row_select(mod_ref, tile_rows, tile_index, n_ctx):
    rows = tile_index * tile_rows + lax.broadcasted_iota(jnp.int32, (tile_rows, 1), 0)
    return jnp.where(rows < n_ctx, mod_ref[1:2, :], mod_ref[0:1, :])


def _cast_kernel(w_ref, o_ref):
    o_ref[...] = w_ref[...].astype(o_ref.dtype)


def _cast_bf16(w):
    n_l, r, c = w.shape
    tr = _tile(r, max(8, (8 * MIB) // (4 * c)), unit=16)
    return pl.pallas_call(
        _cast_kernel, grid=(n_l, r // tr),
        in_specs=[pl.BlockSpec((None, tr, c), lambda l, i: (l, i, 0))],
        out_specs=pl.BlockSpec((None, tr, c), lambda l, i: (l, i, 0)),
        out_shape=jax.ShapeDtypeStruct(w.shape, BF16),
        compiler_params=_params(2, _nbytes((tr, c), F32) + _nbytes((tr, c), BF16)), name="cast_bf16",
    )(w)


W_IN_CAST_ROWS = 512


def _cast_w_in_kernel(a_ref, b_ref, o_ref, oif_ref, *, main_blocks):
    i = pl.program_id(1)
    tr = a_ref.shape[0]

    @pl.when(i < main_blocks)
    def _():
        o_ref[...] = a_ref[...].astype(BF16)

    @pl.when(i >= main_blocks)
    def _():
        o_ref[:tr - IF_WIDTH, :] = a_ref[IF_WIDTH:, :].astype(BF16)
        o_ref[tr - IF_WIDTH:, :] = b_ref[...].astype(BF16)

    @pl.when(i == main_blocks)
    def _():
        oif_ref[...] = jnp.zeros_like(oif_ref)
        oif_ref[:IF_WIDTH, :] = a_ref[:IF_WIDTH, :].astype(BF16)


def _cast_w_in(wt, n_main, n_gate):
    n_l, n, d = wt.shape
    tr = W_IN_CAST_ROWS
    assert n == n_main + IF_WIDTH + n_gate and n_main % tr == 0 and n_gate % tr == 0 and tr % IF_WIDTH == 0
    per = tr // IF_WIDTH
    return pl.pallas_call(
        functools.partial(_cast_w_in_kernel, main_blocks=n_main // tr), grid=(n_l, (n_main + n_gate) // tr),
        in_specs=[pl.BlockSpec((None, tr, d), lambda l, i: (l, i, 0)),
                  pl.BlockSpec((None, IF_WIDTH, d), lambda l, i: (l, per * (i + 1), 0))],
        out_specs=[pl.BlockSpec((None, tr, d), lambda l, i: (l, i, 0)),
                   pl.BlockSpec((None, LANES, d), lambda l, i: (l, 0, 0))],
        out_shape=[jax.ShapeDtypeStruct((n_l, n_main + n_gate, d), BF16), jax.ShapeDtypeStruct((n_l, LANES, d), BF16)],
        compiler_params=_params(2, _nbytes((tr, d), F32) + _nbytes((tr, d), BF16) + _nbytes((LANES, d), BF16)),
        name="cast_w_in",
    )(wt, wt)


def _adaln_kernel(cond_ref, w_ref, b_ref, o_ref):
    c = cond_ref[...]
    s = (c * _sigmoid(c)).astype(BF16)
    acc = jnp.dot(s, w_ref[...].astype(BF16), preferred_element_type=F32)
    o_ref[...] = acc + b_ref[...]


def _adaln(cond, ada_w, ada_b):
    n_layers, d, n = ada_w.shape
    r = cond.shape[0]
    tn = _tile(n, 512)
    blk = _nbytes((d, tn), F32) + _nbytes((r, d), F32) + 2 * _nbytes((r, tn), F32)
    return pl.pallas_call(
        _adaln_kernel,
        grid=(n_layers, n // tn),
        in_specs=[pl.BlockSpec((r, d), lambda l, j: (0, 0)),
                  pl.BlockSpec((None, d, tn), lambda l, j: (l, 0, j)),
                  pl.BlockSpec((None, 1, tn), lambda l, j: (l, 0, j))],
        out_specs=pl.BlockSpec((None, r, tn), lambda l, j: (l, 0, j)),
        out_shape=jax.ShapeDtypeStruct((n_layers, r, n), F32),
        compiler_params=_params(2, blk, _nbytes((d, tn), BF16)), name="adaln",
    )(cond, ada_w, ada_b.reshape(n_layers, 1, n))


ROW_GROUP = 16
COL_CHUNK = 1024


def _inv_rms(x_ref, r0, rows):
    d = x_ref.shape[1]
    cc = min(COL_CHUNK, d)
    ss = None
    for c0 in range(0, d, cc):
        xc = x_ref[r0:r0 + rows, c0:c0 + cc]
        part = jnp.sum(xc * xc, axis=-1, keepdims=True)
        ss = part if ss is None else ss + part
    return lax.rsqrt(ss / d + EPS)


def _norm_mod_kernel(x_ref, g_ref, shift_ref, scale_ref, *rest, n_ctx, routed):
    tm, d = x_ref.shape
    cc = min(COL_CHUNK, d)
    is_ctx = pl.program_id(0) * tm < n_ctx
    shift = jnp.where(is_ctx, shift_ref[1:2, :], shift_ref[0:1, :])
    onep = 1.0 + jnp.where(is_ctx, scale_ref[1:2, :], scale_ref[0:1, :])
    o_ref = rest[1] if routed else rest[0]
    for r0 in range(0, tm, ROW_GROUP):
        inv = _inv_rms(x_ref, r0, ROW_GROUP)
        for c0 in range(0, d, cc):
            cs = slice(c0, c0 + cc)
            y = x_ref[r0:r0 + ROW_GROUP, cs] * inv * g_ref[:, cs]
            o_ref[r0:r0 + ROW_GROUP, cs] = (y * onep[:, cs] + shift[:, cs]).astype(BF16)
    if not routed:
        return
    r_ref, _, gate_ref = rest
    logits = jnp.dot(o_ref[...], r_ref[...], preferred_element_type=F32)
    lane = lax.broadcasted_iota(jnp.int32, logits.shape, 1)
    logits = jnp.where(lane < N_EXPERTS, logits, -jnp.inf)
    m1 = jnp.max(logits, axis=-1, keepdims=True)
    i1 = jnp.min(jnp.where(logits == m1, lane, LANES), axis=-1, keepdims=True)
    rest_l = jnp.where(lane == i1, -jnp.inf, logits)
    m2 = jnp.max(rest_l, axis=-1, keepdims=True)
    i2 = jnp.min(jnp.where(rest_l == m2, lane, LANES), axis=-1, keepdims=True)
    e2 = jnp.exp(m2 - m1)
    gate_ref[...] = jnp.where(lane == i1, 1.0 / (1.0 + e2), 0.0) + jnp.where(lane == i2, e2 / (1.0 + e2), 0.0)


def _norm_mod(x, g, mod, l, k_shift, n_ctx, router=None):
    m, d = x.shape
    tm = _tile(n_ctx, 256, unit=ROW_GROUP)
    assert n_ctx % tm == 0 and m % tm == 0
    in_specs = [pl.BlockSpec((tm, d), lambda i: (i, 0)),
                pl.BlockSpec((None, 1, d), lambda i: (l, 0, 0)),
                pl.BlockSpec((None, MOD_ROWS, d), lambda i: (l, 0, k_shift)),
                pl.BlockSpec((None, MOD_ROWS, d), lambda i: (l, 0, k_shift + 1))]
    blk = _nbytes((tm, d), F32) + _nbytes((tm, d), BF16)
    tmp = 3 * _nbytes((tm, d), F32)
    kern = functools.partial(_norm_mod_kernel, n_ctx=n_ctx, routed=router is not None)
    if router is None:
        return pl.pallas_call(
            kern, grid=(m // tm,), in_specs=in_specs,
            out_specs=pl.BlockSpec((tm, d), lambda i: (i, 0)),
            out_shape=jax.ShapeDtypeStruct((m, d), BF16),
            compiler_params=_params(1, blk, tmp), name="norm_mod",
        )(x, g, mod, mod)
    return pl.pallas_call(
        kern, grid=(m // tm,),
        in_specs=in_specs + [pl.BlockSpec((d, LANES), lambda i: (0, 0))],
        out_specs=[pl.BlockSpec((tm, d), lambda i: (i, 0)), pl.BlockSpec((tm, LANES), lambda i: (i, 0))],
        out_shape=[jax.ShapeDtypeStruct((m, d), BF16), jax.ShapeDtypeStruct((m, LANES), F32)],
        compiler_params=_params(1, blk + _nbytes((d, LANES), BF16), tmp), name="norm_mod_router",
    )(x, g, mod, mod, router)


def _mm_act_kernel(x_ref, wt_ref, *rest, act):
    acc = lax.dot_general(x_ref[...], wt_ref[...], (((1,), (1,)), ((), ())), preferred_element_type=F32)
    if act == "gelu":
        acc = _gelu(acc)
    elif act == "sigmoid":
        acc = _sigmoid(acc)
    elif act == "colscale":
        acc = acc * rest[0][...]
    rest[-1][...] = acc.astype(rest[-1].dtype)


def _mm_act(x, wt, l, row0, n, act, out_dtype, col_scale=None, tm_pref=1280, tn_pref=1024):
    m, k = x.shape
    tm, tn = _tile(m, tm_pref), _tile(n, tn_pref)
    assert row0 % tn == 0
    rb = row0 // tn
    in_specs = [pl.BlockSpec((tm, k), lambda i, j: (i, 0)), pl.BlockSpec((None, tn, k), lambda i, j: (l, rb + j, 0))]
    args = [x, wt]
    if act == "colscale":
        in_specs.append(pl.BlockSpec((1, tn), lambda i, j: (0, j)))
        args.append(col_scale)
    blk = _nbytes((tm, k), BF16) + _nbytes((k, tn), BF16) + _nbytes((tm, tn), out_dtype)
    return pl.pallas_call(
        functools.partial(_mm_act_kernel, act=act),
        grid=(m // tm, n // tn),
        in_specs=in_specs,
        out_specs=pl.BlockSpec((tm, tn), lambda i, j: (i, j)),
        out_shape=jax.ShapeDtypeStruct((m, n), out_dtype),
        compiler_params=_params(2, blk, 2 * _nbytes((tm, tn), F32)), name="mm_" + act,
    )(*args)


def _mm_residual_kernel(z_ref, w_ref, x_ref, g_ref, o_ref, *, n_ctx):
    acc = jnp.dot(z_ref[...], w_ref[...], preferred_element_type=F32)
    gate = _row_select(g_ref, z_ref.shape[0], pl.program_id(0), n_ctx)
    o_ref[...] = x_ref[...] + gate * acc


def _mm_residual(z, w, l, x, mod, l_mod, k_gate, n_ctx, tm_pref=1280, tn_pref=512):
    m, k = z.shape
    n = w.shape[2]
    tm, tn = _tile(m, tm_pref), _tile(n, tn_pref)
    gb = k_gate * (n // tn)
    blk = _nbytes((tm, k), BF16) + _nbytes((k, tn), BF16) + 2 * _nbytes((tm, tn), F32)
    return pl.pallas_call(
        functools.partial(_mm_residual_kernel, n_ctx=n_ctx),
        grid=(m // tm, n // tn),
        in_specs=[pl.BlockSpec((tm, k), lambda i, j: (i, 0)), pl.BlockSpec((None, k, tn), lambda i, j: (l, 0, j)),
                  pl.BlockSpec((tm, tn), lambda i, j: (i, j)),
                  pl.BlockSpec((None, MOD_ROWS, tn), lambda i, j: (l_mod, 0, gb + j))],
        out_specs=pl.BlockSpec((tm, tn), lambda i, j: (i, j)),
        out_shape=jax.ShapeDtypeStruct((m, n), F32),
        compiler_params=_params(2, blk, 2 * _nbytes((tm, tn), F32)), name="mm_residual",
    )(z, w, x, mod)


def _mm_merge_kernel(ta_ref, tb_ref, wa_ref, wb_ref, ga_ref, gb_ref, o_ref):
    ya = jnp.dot(ta_ref[...], wa_ref[...], preferred_element_type=F32)
    yb = jnp.dot(tb_ref[...], wb_ref[...], preferred_element_type=F32)
    o_ref[...] = (ga_ref[...].astype(F32) * ya + gb_ref[...].astype(F32) * yb).astype(o_ref.dtype)


def _mm_merge(t, wa, wb, l, sig, tm_pref=1280, tn_pref=1024):
    m = t.shape[0]
    _, ka, n = wa.shape
    assert wb.shape == wa.shape and t.shape[1] == 2 * ka
    ko = sig.shape[1] - 2 * n
    tm, tn = _tile(m, tm_pref), _tile(n, tn_pref)
    off_a = ko // tn
    off_b = (ko + n) // tn
    assert off_a * tn == ko and off_b * tn == ko + n
    blk = 2 * _nbytes((tm, ka), BF16) + 2 * _nbytes((ka, tn), BF16) + 2 * _nbytes((tm, tn), sig.dtype) \
        + _nbytes((tm, tn), BF16)
    return pl.pallas_call(
        _mm_merge_kernel,
        grid=(m // tm, n // tn),
        in_specs=[pl.BlockSpec((tm, ka), lambda i, j: (i, 0)), pl.BlockSpec((tm, ka), lambda i, j: (i, 1)),
                  pl.BlockSpec((None, ka, tn), lambda i, j: (l, 0, j)), pl.BlockSpec((None, ka, tn), lambda i, j: (l, 0, j)),
                  pl.BlockSpec((tm, tn), lambda i, j: (i, j + off_a)),
                  pl.BlockSpec((tm, tn), lambda i, j: (i, j + off_b))],
        out_specs=pl.BlockSpec((tm, tn), lambda i, j: (i, j)),
        out_shape=jax.ShapeDtypeStruct((m, n), BF16),
        compiler_params=_params(2, blk, 3 * _nbytes((tm, tn), F32)), name="mm_merge",
    )(t, t, wa, wb, sig, sig)


def _ffn_up_kernel(h_ref, wg_ref, wu_ref, *rest, routed):
    h = h_ref[...]
    g = jnp.dot(h, wg_ref[...], preferred_element_type=F32)
    u = jnp.dot(h, wu_ref[...], preferred_element_type=F32)
    act = g * _sigmoid(g) * u
    if routed:
        gates = rest[0][...]
        lane = lax.broadcasted_iota(jnp.int32, gates.shape, 1)
        act = act * jnp.sum(jnp.where(lane == pl.program_id(1), gates, 0.0), axis=-1, keepdims=True)
    rest[-1][...] = act.astype(rest[-1].dtype)


def _ffn_up_dense(h, w13, l, tm_pref=1280, tn_pref=512):
    m, k = h.shape
    f = w13.shape[2] // 2
    tm, tn = _tile(m, tm_pref), _tile(f, tn_pref)
    nb = f // tn
    blk = _nbytes((tm, k), BF16) + 2 * _nbytes((k, tn), BF16) + _nbytes((tm, tn), BF16)
    return pl.pallas_call(
        functools.partial(_ffn_up_kernel, routed=False),
        grid=(m // tm, nb),
        in_specs=[pl.BlockSpec((tm, k), lambda i, j: (i, 0)), pl.BlockSpec((None, k, tn), lambda i, j: (l, 0, j)),
                  pl.BlockSpec((None, k, tn), lambda i, j: (l, 0, j + nb))],
        out_specs=pl.BlockSpec((tm, tn), lambda i, j: (i, j)),
        out_shape=jax.ShapeDtypeStruct((m, f), BF16),
        compiler_params=_params(2, blk, 3 * _nbytes((tm, tn), F32)), name="ffn_up_dense",
    )(h, w13, w13)


def _ffn_up_moe(h, w13, l, gates, tm_pref=1280):
    m, k = h.shape
    two_de = w13.shape[2]
    de = two_de // 2
    tm = _tile(m, tm_pref)
    e0 = l * N_EXPERTS
    blk = _nbytes((tm, k), BF16) + 2 * _nbytes((k, de), BF16) + _nbytes((tm, de), BF16) + _nbytes((tm, LANES), F32)
    return pl.pallas_call(
        functools.partial(_ffn_up_kernel, routed=True),
        grid=(m // tm, N_EXPERTS),
        in_specs=[pl.BlockSpec((tm, k), lambda i, j: (i, 0)), pl.BlockSpec((None, k, de), lambda i, j: (e0 + j, 0, 0)),
                  pl.BlockSpec((None, k, de), lambda i, j: (e0 + j, 0, 1)), pl.BlockSpec((tm, LANES), lambda i, j: (i, 0))],
        out_specs=pl.BlockSpec((tm, de), lambda i, j: (i, j)),
        out_shape=jax.ShapeDtypeStruct((m, N_EXPERTS * de), BF16),
        compiler_params=_params(2, blk, 3 * _nbytes((tm, de), F32)), name="ffn_up_moe",
    )(h, w13, w13, gates)


def _split3(x):
    hi = x.astype(BF16)
    r1 = x - hi.astype(F32)
    mid = r1.astype(BF16)
    lo = (r1 - mid.astype(F32)).astype(BF16)
    return hi, mid, lo


def _mlstm_kernel(qkv_f_ref, qkv_b_ref, pre_f_ref, pre_b_ref, bias_ref, hf_ref, hb_ref, c_ref, n_ref, m_ref):
    @pl.when(pl.program_id(0) == 0)
    def _():
        c_ref[...] = jnp.zeros_like(c_ref)
        n_ref[...] = jnp.zeros_like(n_ref)
        m_ref[...] = jnp.zeros_like(m_ref)

    rows = lax.broadcasted_iota(jnp.int32, (CHUNK, CHUNK), 0)
    cols = lax.broadcasted_iota(jnp.int32, (CHUNK, CHUNK), 1)
    for d, (qkv_ref, pre_ref, out_ref) in enumerate(((qkv_f_ref, pre_f_ref, hf_ref), (qkv_b_ref, pre_b_ref, hb_ref))):
        mask = (rows >= cols) if d == 0 else (rows <= cols)
        tri = jnp.where(mask, 1.0, 0.0).astype(BF16)
        a = pre_ref[...] + bias_ref[...]
        a = GATE_SOFTCAP * jnp.tanh(a / GATE_SOFTCAP)
        lf = _log_sigmoid(a)
        hi, mid, lo = _split3(lf)
        b_all = (jnp.dot(tri, hi, preferred_element_type=F32) + jnp.dot(tri, mid, preferred_element_type=F32)
                 + jnp.dot(tri, lo, preferred_element_type=F32))
        a_t = a.T
        b_t = b_all.T
        last = CHUNK - 1 if d == 0 else 0
        for hd in range(M_HEADS):
            ci = d * 2 * M_HEADS + hd
            cf = ci + M_HEADS
            r = d * M_HEADS + hd
            ig_col, ig_row = a[:, ci:ci + 1], a_t[ci:ci + 1, :]
            b_col, b_row = b_all[:, cf:cf + 1], b_t[cf:cf + 1, :]
            b_last = b_col[last:last + 1, :]
            m_prev = m_ref[r:r + 1, 0:1]
            c_prev = c_ref[r]
            n_prev = n_ref[r:r + 1, :]
            q = qkv_ref[:, hd * M_DK:(hd + 1) * M_DK]
            k = qkv_ref[:, M_HEADS * M_DK + hd * M_DK:M_HEADS * M_DK + (hd + 1) * M_DK]
            v = qkv_ref[:, 2 * M_HEADS * M_DK + hd * M_DV:2 * M_HEADS * M_DK + (hd + 1) * M_DV]

            log_d = jnp.where(mask, b_col - b_row + ig_row, -jnp.inf)
            m_inter = b_col + m_prev
            m_row = jnp.maximum(m_inter, jnp.max(log_d, axis=-1, keepdims=True))
            qk = lax.dot_general(q, k, (((1,), (1,)), ((), ())), preferred_element_type=F32)
            s = qk * jnp.exp(log_d - m_row)
            w_inter = jnp.exp(m_inter - m_row)
            qc = jnp.dot(q, c_prev.astype(BF16), preferred_element_type=F32)
            num = jnp.dot(s.astype(BF16), v, preferred_element_type=F32) + w_inter * qc
            qn = jnp.sum(q.astype(F32) * n_prev, axis=-1, keepdims=True)
            den = jnp.sum(s, axis=-1, keepdims=True) + w_inter * qn
            out_ref[:, hd * M_DV:(hd + 1) * M_DV] = num / jnp.maximum(jnp.abs(den), jnp.exp(-m_row))

            log_w = b_last - b_col + ig_col
            m_new = jnp.maximum(b_last + m_prev, jnp.max(log_w, axis=0, keepdims=True))
            w = jnp.exp(log_w - m_new)
            decay = jnp.exp(b_last + m_prev - m_new)
            wk = w * k.astype(F32)
            c_ref[r] = decay * c_prev + lax.dot_general(wk.astype(BF16), v, (((0,), (0,)), ((), ())),
                                                        preferred_element_type=F32)
            n_ref[r:r + 1, :] = decay * n_prev + jnp.sum(wk, axis=0, keepdims=True)
            m_ref[r:r + 1, :] = jnp.broadcast_to(m_new, (1, LANES))


def _mlstm(qkv, pre, bias, l, n_ctx):
    n = qkv.shape[0]
    nc = n // CHUNK
    ncc = n_ctx // CHUNK
    width = qkv.shape[1]
    hw = M_HEADS * M_DV
    n_chain = N_DIRS * M_HEADS
    fwd = lambda t: (t, 0)
    bwd = lambda t: (jnp.where(t < ncc, ncc - 1 - t, ncc + nc - 1 - t), 0)
    blk = 2 * _nbytes((CHUNK, width), BF16) + 2 * _nbytes((CHUNK, LANES), F32) + 2 * _nbytes((CHUNK, hw), F32)
    return pl.pallas_call(
        _mlstm_kernel,
        grid=(nc,),
        in_specs=[pl.BlockSpec((CHUNK, width), fwd), pl.BlockSpec((CHUNK, width), bwd),
                  pl.BlockSpec((CHUNK, LANES), fwd), pl.BlockSpec((CHUNK, LANES), bwd),
                  pl.BlockSpec((None, 1, LANES), lambda t: (l, 0, 0))],
        out_specs=[pl.BlockSpec((CHUNK, hw), fwd), pl.BlockSpec((CHUNK, hw), bwd)],
        out_shape=[jax.ShapeDtypeStruct((n, hw), F32), jax.ShapeDtypeStruct((n, hw), F32)],
        scratch_shapes=[pltpu.VMEM((n_chain, M_DK, M_DV), F32), pltpu.VMEM((n_chain, M_DK), F32),
                        pltpu.VMEM((n_chain, LANES), F32)],
        compiler_params=_params(1, blk, 10 * MIB), name="mlstm_scan",
    )(qkv, qkv, pre, pre, bias)


def _branch_kernel(gu_ref, hf_ref, hb_ref, osig_ref, sgu_g_ref, sgu_w_ref, sgu_bt_ref, mh_g_ref, o_ref, *, a_width):
    gd = a_width // A_GROUPS
    for r0 in range(0, gu_ref.shape[0], CHUNK):
        rs = slice(r0, r0 + CHUNK)
        v = gu_ref[rs, a_width:2 * a_width].astype(F32)
        vc = v - jnp.mean(v, axis=-1, keepdims=True)
        vn = (vc * lax.rsqrt(jnp.mean(vc * vc, axis=-1, keepdims=True) + EPS) * sgu_g_ref[...]).astype(BF16)
        for g in range(A_GROUPS):
            sl = slice(g * gd, (g + 1) * gd)
            s = jnp.dot(sgu_w_ref[g], vn[:, sl], preferred_element_type=F32) + sgu_bt_ref[:, g:g + 1]
            o_ref[rs, sl] = (gu_ref[rs, sl].astype(F32) * s).astype(o_ref.dtype)
        for hd in range(M_HEADS):
            sl = slice(hd * M_DV, (hd + 1) * M_DV)
            h = hf_ref[rs, sl] + hb_ref[rs, sl]
            hn = h * lax.rsqrt(jnp.mean(h * h, axis=-1, keepdims=True) + EPS)
            o_ref[rs, a_width + hd * M_DV:a_width + (hd + 1) * M_DV] = \
                (hn * mh_g_ref[:, sl] * osig_ref[rs, sl].astype(F32)).astype(o_ref.dtype)


def _branch_inputs(gu, hf, hb, sig, sgu_g, sgu_w, sgu_bt, mh_g, l, a_width):
    n = gu.shape[0]
    hw = M_HEADS * M_DV
    tr = 2 * CHUNK if n % (2 * CHUNK) == 0 else CHUNK
    blk = _nbytes((tr, 2 * a_width), gu.dtype) + 2 * _nbytes((tr, hw), F32) + _nbytes((tr, hw), sig.dtype) \
        + _nbytes((tr, a_width + hw), BF16)
    return pl.pallas_call(
        functools.partial(_branch_kernel, a_width=a_width),
        grid=(n // tr,),
        in_specs=[pl.BlockSpec((tr, 2 * a_width), lambda t: (t, 0)),
                  pl.BlockSpec((tr, hw), lambda t: (t, 0)), pl.BlockSpec((tr, hw), lambda t: (t, 0)),
                  pl.BlockSpec((tr, hw), lambda t: (t, 0)),
                  pl.BlockSpec((None, 1, a_width), lambda t: (l, 0, 0)),
                  pl.BlockSpec((None, A_GROUPS, CHUNK, CHUNK), lambda t: (l, 0, 0, 0)),
                  pl.BlockSpec((None, CHUNK, LANES), lambda t: (l, 0, 0)),
                  pl.BlockSpec((None, 1, hw), lambda t: (l, 0, 0))],
        out_specs=pl.BlockSpec((tr, a_width + hw), lambda t: (t, 0)),
        out_shape=jax.ShapeDtypeStruct((n, a_width + hw), BF16),
        compiler_params=_params(1, blk, 6 * MIB), name="branch_inputs",
    )(gu, hf, hb, sig, sgu_g, sgu_w, sgu_bt, mh_g)


def _final_norm_kernel(x_ref, g_ref, o_ref):
    tm, d = x_ref.shape
    cc = min(COL_CHUNK, d)
    for r0 in range(0, tm, ROW_GROUP):
        inv = _inv_rms(x_ref, r0, ROW_GROUP)
        for c0 in range(0, d, cc):
            cs = slice(c0, c0 + cc)
            o_ref[r0:r0 + ROW_GROUP, cs] = x_ref[r0:r0 + ROW_GROUP, cs] * inv * g_ref[:, cs]


def _final_norm(x, g, n_ctx):
    m, d = x.shape
    tm = _tile(n_ctx, 256)
    skip = n_ctx // tm
    assert skip * tm == n_ctx and (m - n_ctx) % tm == 0
    return pl.pallas_call(
        _final_norm_kernel, grid=((m - n_ctx) // tm,),
        in_specs=[pl.BlockSpec((tm, d), lambda i: (i + skip, 0)), pl.BlockSpec((1, d), lambda i: (0, 0))],
        out_specs=pl.BlockSpec((tm, d), lambda i: (i, 0)),
        out_shape=jax.ShapeDtypeStruct((m - n_ctx, d), F32),
        compiler_params=_params(1, 2 * _nbytes((tm, d), F32), 2 * _nbytes((tm, d), F32)), name="final_norm",
    )(x, g)


def kernel(x, c, ctx, c_ctx, ada_w, ada_b, norm_mix, w_in, sgu_norm, sgu_w, sgu_b, gate_bias, mh_norm, w_br_a, w_br_b, w_out, norm_ffn, dense_w13, dense_w2, moe_router, moe_w13, moe_w2, final_norm):
    bsz, seq, d = x.shape
    n_ctx = ctx.shape[1]
    assert bsz == 1 and seq % CHUNK == 0 and n_ctx % CHUNK == 0
    depth = w_in.shape[0]
    a_width = sgu_norm.shape[1]
    qk_w = M_HEADS * M_DK
    hw = M_HEADS * M_DV
    n_main = 2 * a_width + 2 * qk_w + 2 * hw
    n_gate = 2 * d
    assert mh_norm.shape[1] == hw and a_width == hw and w_in.shape[2] == n_main + IF_WIDTH + n_gate
    o_q, o_o = 2 * a_width, 2 * a_width + 2 * qk_w + hw

    w_main, w_if = _cast_w_in(jnp.swapaxes(w_in, 1, 2), n_main, n_gate)
    wa, wb, wo = _cast_bf16(w_br_a), _cast_bf16(w_br_b), _cast_bf16(w_out)
    w13_d, w2_d = _cast_bf16(dense_w13), _cast_bf16(dense_w2)
    w13_m = _cast_bf16(moe_w13.reshape((-1,) + moe_w13.shape[2:]))
    w2_m = _cast_bf16(moe_w2.reshape(moe_w2.shape[0], -1, d))
    sgu_wb = _cast_bf16(sgu_w.reshape(depth, A_GROUPS * CHUNK, CHUNK)).reshape(depth, A_GROUPS, CHUNK, CHUNK)
    router = jnp.pad(moe_router, ((0, 0), (0, 0), (0, LANES - N_EXPERTS))).astype(BF16)

    cond = jnp.zeros((MOD_ROWS, d), F32).at[0].set(c[0]).at[1].set(c_ctx)
    mod = _adaln(cond, ada_w, ada_b)

    q_scale = jnp.concatenate([jnp.full((1, qk_w), M_DK ** -0.5, F32), jnp.ones((1, qk_w + hw), F32)], axis=1)
    bias_if = jnp.pad(gate_bias, ((0, 0), (0, LANES - IF_WIDTH))).reshape(depth, 1, LANES)
    sgu_bt = jnp.pad(jnp.swapaxes(sgu_b, 1, 2), ((0, 0), (0, 0), (0, LANES - A_GROUPS)))
    g_mix, g_ffn = norm_mix.reshape(depth, 1, d), norm_ffn.reshape(depth, 1, d)
    g_sgu, g_mh = sgu_norm.reshape(depth, 1, a_width), mh_norm.reshape(depth, 1, hw)

    xs = jnp.concatenate([ctx.reshape(n_ctx, d), x.reshape(seq, d)], axis=0)
    for l in range(depth):
        h = _norm_mod(xs, g_mix, mod, l, 0, n_ctx)
        gu = _mm_act(h, w_main, l, 0, o_q, "gelu", BF16)
        qkv = _mm_act(h, w_main, l, o_q, o_o - o_q, "colscale", BF16, col_scale=q_scale)
        sig = _mm_act(h, w_main, l, o_o, hw + n_gate, "sigmoid", BF16)
        pre = _mm_act(h, w_if, l, 0, LANES, "none", F32)
        hf, hb = _mlstm(qkv, pre, bias_if, l, n_ctx)
        t = _branch_inputs(gu, hf, hb, sig, g_sgu, sgu_wb, sgu_bt, g_mh, l, a_width)
        z = _mm_merge(t, wa, wb, l, sig)
        xs = _mm_residual(z, wo, l, xs, mod, l, 2, n_ctx)
        if l % 2 == 0:
            h2 = _norm_mod(xs, g_ffn, mod, l, 3, n_ctx)
            act = _ffn_up_dense(h2, w13_d, l // 2)
            xs = _mm_residual(act, w2_d, l // 2, xs, mod, l, 5, n_ctx)
        else:
            h2, gates = _norm_mod(xs, g_ffn, mod, l, 3, n_ctx, router=router[l // 2])
            act = _ffn_up_moe(h2, w13_m, l // 2, gates)
            xs = _mm_residual(act, w2_m, l // 2, xs, mod, l, 5, n_ctx)
    return _final_norm(xs, final_norm.reshape(1, d), n_ctx).reshape(bsz, seq, d)
```

```python
import functools
import math

import jax
import jax.numpy as jnp
from jax import lax
from jax.experimental import pallas as pl
from jax.experimental.pallas import tpu as pltpu

F32 = jnp.float32
BF16 = jnp.bfloat16

CHUNK = 128
A_GROUPS = 4
M_HEADS = 4
M_DK = 128
M_DV = 256
N_DIRS = 2
IF_WIDTH = N_DIRS * 2 * M_HEADS
GATE_SOFTCAP = 15.0
N_EXPERTS = 8
N_MOD = 6
EPS = 1e-6
MOD_ROWS = 8

LANES = 128
BF16_SUBLANES = 16
V7X_VMEM_BYTES = 64 * 1024 * 1024
VMEM_LIMIT_CAP = V7X_VMEM_BYTES - 3 * 1024 * 1024
MIB = 1024 * 1024


def _params(n_grid_dims, block_bytes, temp_bytes=0):
    need = 2 * block_bytes + temp_bytes + 8 * MIB
    return pltpu.CompilerParams(
        dimension_semantics=("arbitrary",) * n_grid_dims,
        vmem_limit_bytes=int(min(max(need, 16 * MIB), VMEM_LIMIT_CAP)))


def _nbytes(shape, dtype):
    return math.prod(shape) * jnp.dtype(dtype).itemsize


def _tile(n, pref, unit=LANES):
    if n <= pref:
        return n
    t = pref - pref % unit
    while n % t:
        t -= unit
    assert t > 0
    return t


def _sigmoid(x):
    return 1.0 / (1.0 + jnp.exp(-x))


def _sigmoid_tanh(x):
    return 0.5 + 0.5 * jnp.tanh(0.5 * x)


def _gelu(x):
    return 0.5 * x * (1.0 + jnp.tanh(0.7978845608028654 * (x + 0.044715 * (x * x * x))))


def _log_sigmoid(x):
    return jnp.minimum(x, 0.0) - jnp.log1p(jnp.exp(-jnp.abs(x)))


def _row_select(mod_ref, tile_rows, tile_index, n_ctx):
    rows = tile_index * tile_rows + lax.broadcasted_iota(jnp.int32, (tile_rows, 1), 0)
    return jnp.where(rows < n_ctx, mod_ref[1:2, :], mod_ref[0:1, :])


def _wspec(w, l, block, index):
    if w.ndim == 2:
        return pl.BlockSpec(block, index)
    return pl.BlockSpec((None,) + block, lambda *g: (l,) + index(*g))


def _cast_kernel(w_ref, o_ref):
    o_ref[...] = w_ref[...].astype(o_ref.dtype)


def _cast_bf16(w, l):
    _, r, c = w.shape
    tr = _tile(r, max(BF16_SUBLANES, (8 * MIB) // (4 * c)), unit=BF16_SUBLANES)
    return pl.pallas_call(
        _cast_kernel, grid=(r // tr,),
        in_specs=[pl.BlockSpec((None, tr, c), lambda i: (l, i, 0))],
        out_specs=pl.BlockSpec((tr, c), lambda i: (i, 0)),
        out_shape=jax.ShapeDtypeStruct((r, c), BF16),
        compiler_params=_params(1, _nbytes((tr, c), F32) + _nbytes((tr, c), BF16)), name="cast_bf16",
    )(w)


def _cast_w_if_kernel(w_ref, o_ref):
    o_ref[...] = jnp.zeros_like(o_ref)
    o_ref[:IF_WIDTH, :] = w_ref[...].astype(BF16)


def _cast_w_if(wt, n_main):
    n_l, _, d = wt.shape
    assert n_main % IF_WIDTH == 0
    return pl.pallas_call(
        _cast_w_if_kernel, grid=(n_l,),
        in_specs=[pl.BlockSpec((None, IF_WIDTH, d), lambda l: (l, n_main // IF_WIDTH, 0))],
        out_specs=pl.BlockSpec((None, LANES, d), lambda l: (l, 0, 0)),
        out_shape=jax.ShapeDtypeStruct((n_l, LANES, d), BF16),
        compiler_params=_params(1, _nbytes((LANES, d), F32)), name="cast_w_if",
    )(wt)


def _w_in_rows(a_ref, b_ref, o_ref, is_gate):
    tr = a_ref.shape[0]
    keep = tr - IF_WIDTH
    o_ref[:keep, :] = jnp.where(is_gate, a_ref[IF_WIDTH:, :], a_ref[:keep, :]).astype(BF16)
    o_ref[keep:, :] = jnp.where(is_gate, b_ref[...], a_ref[keep:, :]).astype(BF16)


def _w_in_specs(wt, l, n_main, n_gate, tr, chunk_of):
    _, n, d = wt.shape
    assert n == n_main + IF_WIDTH + n_gate and n_main % tr == 0 and n_gate % tr == 0 and tr % IF_WIDTH == 0
    per = tr // IF_WIDTH
    ins = [pl.BlockSpec((None, tr, d), lambda *g: (l, chunk_of(*g), 0)),
           pl.BlockSpec((None, IF_WIDTH, d), lambda *g: (l, per * (chunk_of(*g) + 1), 0))]
    out = pl.BlockSpec((tr, d), lambda *g: (chunk_of(*g), 0))
    return ins, out, jax.ShapeDtypeStruct((n_main + n_gate, d), BF16)


def _cast_w_in_kernel(a_ref, b_ref, o_ref, *, main_chunks):
    _w_in_rows(a_ref, b_ref, o_ref, pl.program_id(0) >= main_chunks)


def _cast_w_in(wt, l, n_main, n_gate):
    tr = 512
    ins, out, shape = _w_in_specs(wt, l, n_main, n_gate, tr, lambda i: i)
    d = wt.shape[2]
    return pl.pallas_call(
        functools.partial(_cast_w_in_kernel, main_chunks=n_main // tr), grid=((n_main + n_gate) // tr,),
        in_specs=ins, out_specs=out, out_shape=shape,
        compiler_params=_params(1, _nbytes((tr, d), F32) + _nbytes((tr, d), BF16)), name="cast_w_in",
    )(wt, wt)


def _ride_rows(r, n_steps):
    for cr in range(BF16_SUBLANES, r + 1, BF16_SUBLANES):
        if r % cr == 0 and r // cr <= n_steps:
            return cr
    return None


def _pallas(kern, grid, in_specs, args, out_specs, out_shapes, blk, tmp, name, ride=None):
    n_out = len(out_specs)

    def call(k, ins, outs, shapes, a, extra_blk=0):
        return pl.pallas_call(k, grid=grid, in_specs=ins, out_specs=outs, out_shape=shapes,
                              compiler_params=_params(len(grid), blk + extra_blk, tmp), name=name)(*a)

    if ride is None:
        return call(kern, in_specs, out_specs, out_shapes, args)
    n_steps = math.prod(grid)
    strides = [math.prod(grid[d + 1:]) for d in range(len(grid))]
    n_in = len(in_specs)

    def step(*g):
        return sum(gi * st for gi, st in zip(g, strides))

    if ride[0] == "plain":
        _, src, ls = ride
        _, r, c = src.shape
        cr = _ride_rows(r, n_steps)
        if cr is None:
            return (*call(kern, in_specs, out_specs, out_shapes, args), _cast_bf16(src, ls))
        last = r // cr - 1

        def chunk(*g):
            return jnp.minimum(step(*g), last)

        def riding(*refs):
            src_ref, dst_ref = refs[n_in], refs[n_in + 1 + n_out]
            dst_ref[...] = src_ref[...].astype(BF16)
            kern(*refs[:n_in], *refs[n_in + 1:n_in + 1 + n_out], *refs[n_in + 2 + n_out:])

        return call(riding, in_specs + [pl.BlockSpec((None, cr, c), lambda *g: (ls, chunk(*g), 0))],
                    out_specs + [pl.BlockSpec((cr, c), lambda *g: (chunk(*g), 0))],
                    out_shapes + [jax.ShapeDtypeStruct((r, c), BF16)], args + [src],
                    _nbytes((cr, c), F32) + _nbytes((cr, c), BF16))

    _, wt, ls, n_main, n_gate = ride
    d = wt.shape[2]
    tr = next((t for t in range(IF_WIDTH, n_main + 1, IF_WIDTH)
               if n_main % t == 0 and n_gate % t == 0 and (n_main + n_gate) // t <= n_steps), None)
    if tr is None:
        return (*call(kern, in_specs, out_specs, out_shapes, args), _cast_w_in(wt, ls, n_main, n_gate))
    last = (n_main + n_gate) // tr - 1
    main_chunks = n_main // tr

    def chunk(*g):
        return jnp.minimum(step(*g), last)

    ins, out, shape = _w_in_specs(wt, ls, n_main, n_gate, tr, chunk)

    def riding(*refs):
        a_ref, b_ref, dst_ref = refs[n_in], refs[n_in + 1], refs[n_in + 2 + n_out]
        ids = [pl.program_id(dd) for dd in range(len(grid))]
        _w_in_rows(a_ref, b_ref, dst_ref, chunk(*ids) >= main_chunks)
        kern(*refs[:n_in], *refs[n_in + 2:n_in + 2 + n_out], *refs[n_in + 3 + n_out:])

    return call(riding, in_specs + ins, out_specs + [out], out_shapes + [shape], args + [wt, wt],
                _nbytes((tr, d), F32) + _nbytes((tr, d), BF16))


def _adaln_kernel(cond_ref, w_ref, b_ref, o_ref):
    c = cond_ref[...]
    s = (c * _sigmoid(c)).astype(BF16)
    acc = jnp.dot(s, w_ref[...].astype(BF16), preferred_element_type=F32)
    o_ref[...] = acc + b_ref[...]


def _adaln(cond, ada_w, ada_b):
    n_layers, d, n = ada_w.shape
    r = cond.shape[0]
    tn = _tile(n, 512)
    blk = _nbytes((d, tn), F32) + _nbytes((r, d), F32) + 2 * _nbytes((r, tn), F32)
    return pl.pallas_call(
        _adaln_kernel,
        grid=(n_layers, n // tn),
        in_specs=[pl.BlockSpec((r, d), lambda l, j: (0, 0)),
                  pl.BlockSpec((None, d, tn), lambda l, j: (l, 0, j)),
                  pl.BlockSpec((None, 1, tn), lambda l, j: (l, 0, j))],
        out_specs=pl.BlockSpec((None, r, tn), lambda l, j: (l, 0, j)),
        out_shape=jax.ShapeDtypeStruct((n_layers, r, n), F32),
        compiler_params=_params(2, blk, _nbytes((d, tn), BF16)), name="adaln",
    )(cond, ada_w, ada_b.reshape(n_layers, 1, n))


ROW_GROUP = 16
COL_CHUNK = 1024


def _inv_rms(x_ref, r0, rows):
    d = x_ref.shape[1]
    cc = min(COL_CHUNK, d)
    ss = None
    for c0 in range(0, d, cc):
        xc = x_ref[r0:r0 + rows, c0:c0 + cc]
        part = jnp.sum(xc * xc, axis=-1, keepdims=True)
        ss = part if ss is None else ss + part
    return lax.rsqrt(ss / d + EPS)


def _norm_mod_kernel(x_ref, g_ref, shift_ref, scale_ref, *rest, n_ctx, routed):
    tm, d = x_ref.shape
    cc = min(COL_CHUNK, d)
    is_ctx = pl.program_id(0) * tm < n_ctx
    shift = jnp.where(is_ctx, shift_ref[1:2, :], shift_ref[0:1, :])
    onep = 1.0 + jnp.where(is_ctx, scale_ref[1:2, :], scale_ref[0:1, :])
    o_ref = rest[1] if routed else rest[0]
    for r0 in range(0, tm, ROW_GROUP):
        inv = _inv_rms(x_ref, r0, ROW_GROUP)
        for c0 in range(0, d, cc):
            cs = slice(c0, c0 + cc)
            y = x_ref[r0:r0 + ROW_GROUP, cs] * inv * g_ref[:, cs]
            o_ref[r0:r0 + ROW_GROUP, cs] = (y * onep[:, cs] + shift[:, cs]).astype(BF16)
    if not routed:
        return
    r_ref, _, gate_ref = rest
    logits = jnp.dot(o_ref[...], r_ref[...], preferred_element_type=F32)
    lane = lax.broadcasted_iota(jnp.int32, logits.shape, 1)
    logits = jnp.where(lane < N_EXPERTS, logits, -jnp.inf)
    m1 = jnp.max(logits, axis=-1, keepdims=True)
    i1 = jnp.min(jnp.where(logits == m1, lane, LANES), axis=-1, keepdims=True)
    rest_l = jnp.where(lane == i1, -jnp.inf, logits)
    m2 = jnp.max(rest_l, axis=-1, keepdims=True)
    i2 = jnp.min(jnp.where(rest_l == m2, lane, LANES), axis=-1, keepdims=True)
    e2 = jnp.exp(m2 - m1)
    gate_ref[...] = jnp.where(lane == i1, 1.0 / (1.0 + e2), 0.0) + jnp.where(lane == i2, e2 / (1.0 + e2), 0.0)


def _norm_mod(x, g, mod, l, k_shift, n_ctx, router=None):
    m, d = x.shape
    tm = _tile(n_ctx, 256, unit=ROW_GROUP)
    assert n_ctx % tm == 0 and m % tm == 0
    in_specs = [pl.BlockSpec((tm, d), lambda i: (i, 0)),
                pl.BlockSpec((None, 1, d), lambda i: (l, 0, 0)),
                pl.BlockSpec((None, MOD_ROWS, d), lambda i: (l, 0, k_shift)),
                pl.BlockSpec((None, MOD_ROWS, d), lambda i: (l, 0, k_shift + 1))]
    blk = _nbytes((tm, d), F32) + _nbytes((tm, d), BF16)
    tmp = 3 * _nbytes((tm, d), F32)
    kern = functools.partial(_norm_mod_kernel, n_ctx=n_ctx, routed=router is not None)
    if router is None:
        return pl.pallas_call(
            kern, grid=(m // tm,), in_specs=in_specs,
            out_specs=pl.BlockSpec((tm, d), lambda i: (i, 0)),
            out_shape=jax.ShapeDtypeStruct((m, d), BF16),
            compiler_params=_params(1, blk, tmp), name="norm_mod",
        )(x, g, mod, mod)
    return pl.pallas_call(
        kern, grid=(m // tm,),
        in_specs=in_specs + [pl.BlockSpec((d, LANES), lambda i: (0, 0))],
        out_specs=[pl.BlockSpec((tm, d), lambda i: (i, 0)), pl.BlockSpec((tm, LANES), lambda i: (i, 0))],
        out_shape=[jax.ShapeDtypeStruct((m, d), BF16), jax.ShapeDtypeStruct((m, LANES), F32)],
        compiler_params=_params(1, blk + _nbytes((d, LANES), BF16), tmp), name="norm_mod_router",
    )(x, g, mod, mod, router)


def _mm_act_kernel(x_ref, wt_ref, *rest, act):
    acc = lax.dot_general(x_ref[...], wt_ref[...], (((1,), (1,)), ((), ())), preferred_element_type=F32)
    if act == "gelu":
        acc = _gelu(acc)
    elif act == "sigmoid":
        acc = _sigmoid_tanh(acc)
    elif act == "colscale":
        acc = acc * rest[0][...]
    rest[-1][...] = acc.astype(rest[-1].dtype)


def _mm_act(x, wt, l, row0, n, act, out_dtype, col_scale=None, ride=None, tm_pref=1280, tn_pref=1024):
    m, k = x.shape
    tm, tn = _tile(m, tm_pref), _tile(n, tn_pref)
    assert row0 % tn == 0
    rb = row0 // tn
    in_specs = [pl.BlockSpec((tm, k), lambda i, j: (i, 0)), _wspec(wt, l, (tn, k), lambda i, j: (rb + j, 0))]
    args = [x, wt]
    if act == "colscale":
        in_specs.append(pl.BlockSpec((1, tn), lambda i, j: (0, j)))
        args.append(col_scale)
    blk = _nbytes((tm, k), BF16) + _nbytes((tn, k), BF16) + _nbytes((tm, tn), out_dtype)
    return _pallas(functools.partial(_mm_act_kernel, act=act), (m // tm, n // tn), in_specs, args,
                   [pl.BlockSpec((tm, tn), lambda i, j: (i, j))], [jax.ShapeDtypeStruct((m, n), out_dtype)],
                   blk, 2 * _nbytes((tm, tn), F32), "mm_" + act, ride)


def _mm_residual_kernel(z_ref, w_ref, x_ref, g_ref, o_ref, *, n_ctx):
    acc = jnp.dot(z_ref[...], w_ref[...], preferred_element_type=F32)
    gate = _row_select(g_ref, z_ref.shape[0], pl.program_id(0), n_ctx)
    o_ref[...] = x_ref[...] + gate * acc


def _mm_residual(z, w, x, mod, l_mod, k_gate, n_ctx, ride=None, tm_pref=1280, tn_pref=512):
    m, k = z.shape
    n = w.shape[1]
    tm, tn = _tile(m, tm_pref), _tile(n, tn_pref)
    gb = k_gate * (n // tn)
    blk = _nbytes((tm, k), BF16) + _nbytes((k, tn), BF16) + 2 * _nbytes((tm, tn), F32)
    in_specs = [pl.BlockSpec((tm, k), lambda i, j: (i, 0)), pl.BlockSpec((k, tn), lambda i, j: (0, j)),
                pl.BlockSpec((tm, tn), lambda i, j: (i, j)),
                pl.BlockSpec((None, MOD_ROWS, tn), lambda i, j: (l_mod, 0, gb + j))]
    return _pallas(functools.partial(_mm_residual_kernel, n_ctx=n_ctx), (m // tm, n // tn), in_specs, [z, w, x, mod],
                   [pl.BlockSpec((tm, tn), lambda i, j: (i, j))], [jax.ShapeDtypeStruct((m, n), F32)],
                   blk, 2 * _nbytes((tm, tn), F32), "mm_residual", ride)


def _mm_merge_kernel(ta_ref, tb_ref, wa_ref, wb_ref, ga_ref, gb_ref, o_ref):
    ya = jnp.dot(ta_ref[...], wa_ref[...], preferred_element_type=F32)
    yb = jnp.dot(tb_ref[...], wb_ref[...], preferred_element_type=F32)
    o_ref[...] = (ga_ref[...].astype(F32) * ya + gb_ref[...].astype(F32) * yb).astype(o_ref.dtype)


def _mm_merge(t, wa, wb, sig, tm_pref=1280, tn_pref=1024):
    m = t.shape[0]
    ka, n = wa.shape
    assert wb.shape == wa.shape and t.shape[1] == 2 * ka
    ko = sig.shape[1] - 2 * n
    tm, tn = _tile(m, tm_pref), _tile(n, tn_pref)
    off_a = ko // tn
    off_b = (ko + n) // tn
    assert off_a * tn == ko and off_b * tn == ko + n
    blk = 2 * _nbytes((tm, ka), BF16) + 2 * _nbytes((ka, tn), BF16) + 2 * _nbytes((tm, tn), sig.dtype) \
        + _nbytes((tm, tn), BF16)
    return pl.pallas_call(
        _mm_merge_kernel,
        grid=(m // tm, n // tn),
        in_specs=[pl.BlockSpec((tm, ka), lambda i, j: (i, 0)), pl.BlockSpec((tm, ka), lambda i, j: (i, 1)),
                  pl.BlockSpec((ka, tn), lambda i, j: (0, j)), pl.BlockSpec((ka, tn), lambda i, j: (0, j)),
                  pl.BlockSpec((tm, tn), lambda i, j: (i, j + off_a)),
                  pl.BlockSpec((tm, tn), lambda i, j: (i, j + off_b))],
        out_specs=pl.BlockSpec((tm, tn), lambda i, j: (i, j)),
        out_shape=jax.ShapeDtypeStruct((m, n), BF16),
        compiler_params=_params(2, blk, 3 * _nbytes((tm, tn), F32)), name="mm_merge",
    )(t, t, wa, wb, sig, sig)


def _ffn_up_kernel(h_ref, wg_ref, wu_ref, *rest, routed):
    h = h_ref[...]
    g = jnp.dot(h, wg_ref[...], preferred_element_type=F32)
    u = jnp.dot(h, wu_ref[...], preferred_element_type=F32)
    act = g * _sigmoid_tanh(g) * u
    if routed:
        gates = rest[0][...]
        lane = lax.broadcasted_iota(jnp.int32, gates.shape, 1)
        act = act * jnp.sum(jnp.where(lane == pl.program_id(1), gates, 0.0), axis=-1, keepdims=True)
    rest[-1][...] = act.astype(rest[-1].dtype)


def _ffn_up_dense(h, w13, ride=None, tm_pref=1280, tn_pref=512):
    m, k = h.shape
    f = w13.shape[1] // 2
    tm, tn = _tile(m, tm_pref), _tile(f, tn_pref)
    nb = f // tn
    blk = _nbytes((tm, k), BF16) + 2 * _nbytes((k, tn), BF16) + _nbytes((tm, tn), BF16)
    in_specs = [pl.BlockSpec((tm, k), lambda i, j: (i, 0)), pl.BlockSpec((k, tn), lambda i, j: (0, j)),
                pl.BlockSpec((k, tn), lambda i, j: (0, j + nb))]
    return _pallas(functools.partial(_ffn_up_kernel, routed=False), (m // tm, nb), in_specs, [h, w13, w13],
                   [pl.BlockSpec((tm, tn), lambda i, j: (i, j))], [jax.ShapeDtypeStruct((m, f), BF16)],
                   blk, 3 * _nbytes((tm, tn), F32), "ffn_up_dense", ride)


def _ffn_up_moe(h, w13, gates, ride=None, tm_pref=1280):
    m, k = h.shape
    two_de = w13.shape[2]
    de = two_de // 2
    tm = _tile(m, tm_pref)
    blk = _nbytes((tm, k), BF16) + 2 * _nbytes((k, de), BF16) + _nbytes((tm, de), BF16) + _nbytes((tm, LANES), F32)
    in_specs = [pl.BlockSpec((tm, k), lambda i, j: (i, 0)), pl.BlockSpec((None, k, de), lambda i, j: (j, 0, 0)),
                pl.BlockSpec((None, k, de), lambda i, j: (j, 0, 1)), pl.BlockSpec((tm, LANES), lambda i, j: (i, 0))]
    return _pallas(functools.partial(_ffn_up_kernel, routed=True), (m // tm, N_EXPERTS), in_specs, [h, w13, w13, gates],
                   [pl.BlockSpec((tm, de), lambda i, j: (i, j))], [jax.ShapeDtypeStruct((m, N_EXPERTS * de), BF16)],
                   blk, 3 * _nbytes((tm, de), F32), "ffn_up_moe", ride)


def _split3(x):
    hi = x.astype(BF16)
    r1 = x - hi.astype(F32)
    mid = r1.astype(BF16)
    lo = (r1 - mid.astype(F32)).astype(BF16)
    return hi, mid, lo


def _mlstm_step(qkv_f_ref, qkv_b_ref, pre_f_ref, pre_b_ref, bias_ref, hf_ref, hb_ref, c_ref, n_ref, m_ref):
    rows = lax.broadcasted_iota(jnp.int32, (CHUNK, CHUNK), 0)
    cols = lax.broadcasted_iota(jnp.int32, (CHUNK, CHUNK), 1)
    for d, (qkv_ref, pre_ref, out_ref) in enumerate(((qkv_f_ref, pre_f_ref, hf_ref), (qkv_b_ref, pre_b_ref, hb_ref))):
        mask = (rows >= cols) if d == 0 else (rows <= cols)
        tri = jnp.where(mask, 1.0, 0.0).astype(BF16)
        a = pre_ref[...] + bias_ref[...]
        a = GATE_SOFTCAP * jnp.tanh(a / GATE_SOFTCAP)
        lf = _log_sigmoid(a)
        hi, mid, lo = _split3(lf)
        b_all = (jnp.dot(tri, hi, preferred_element_type=F32) + jnp.dot(tri, mid, preferred_element_type=F32)
                 + jnp.dot(tri, lo, preferred_element_type=F32))
        a_t = a.T
        b_t = b_all.T
        last = CHUNK - 1 if d == 0 else 0
        for hd in range(M_HEADS):
            ci = d * 2 * M_HEADS + hd
            cf = ci + M_HEADS
            r = d * M_HEADS + hd
            ig_col, ig_row = a[:, ci:ci + 1], a_t[ci:ci + 1, :]
            b_col, b_row = b_all[:, cf:cf + 1], b_t[cf:cf + 1, :]
            b_last = b_col[last:last + 1, :]
            m_prev = m_ref[r:r + 1, 0:1]
            c_prev = c_ref[r]
            n_prev = n_ref[r:r + 1, :]
            q = qkv_ref[:, hd * M_DK:(hd + 1) * M_DK]
            k = qkv_ref[:, M_HEADS * M_DK + hd * M_DK:M_HEADS * M_DK + (hd + 1) * M_DK]
            v = qkv_ref[:, 2 * M_HEADS * M_DK + hd * M_DV:2 * M_HEADS * M_DK + (hd + 1) * M_DV]

            log_d = jnp.where(mask, b_col - b_row + ig_row, -jnp.inf)
            m_inter = b_col + m_prev
            m_row = jnp.maximum(m_inter, jnp.max(log_d, axis=-1, keepdims=True))
            qk = lax.dot_general(q, k, (((1,), (1,)), ((), ())), preferred_element_type=F32)
            s = qk * jnp.exp(log_d - m_row)
            w_inter = jnp.exp(m_inter - m_row)
            qc = jnp.dot(q, c_prev.astype(BF16), preferred_element_type=F32)
            num = jnp.dot(s.astype(BF16), v, preferred_element_type=F32) + w_inter * qc
            qn = jnp.sum(q.astype(F32) * n_prev, axis=-1, keepdims=True)
            den = jnp.sum(s, axis=-1, keepdims=True) + w_inter * qn
            out_ref[:, hd * M_DV:(hd + 1) * M_DV] = num / jnp.maximum(jnp.abs(den), jnp.exp(-m_row))

            log_w = b_last - b_col + ig_col
            m_new = jnp.maximum(b_last + m_prev, jnp.max(log_w, axis=0, keepdims=True))
            w = jnp.exp(log_w - m_new)
            decay = jnp.exp(b_last + m_prev - m_new)
            wk = w * k.astype(F32)
            c_ref[r] = decay * c_prev + lax.dot_general(wk.astype(BF16), v, (((0,), (0,)), ((), ())),
                                                        preferred_element_type=F32)
            n_ref[r:r + 1, :] = decay * n_prev + jnp.sum(wk, axis=0, keepdims=True)
            m_ref[r:r + 1, :] = jnp.broadcast_to(m_new, (1, LANES))


SCAN_STATE_SHAPES = ((N_DIRS * M_HEADS, M_DK, M_DV), (N_DIRS * M_HEADS, M_DK), (N_DIRS * M_HEADS, LANES))


def _mlstm_kernel(qkv_f_ref, qkv_b_ref, pre_f_ref, pre_b_ref, bias_ref, hf_ref, hb_ref, c_ref, n_ref, m_ref):
    @pl.when(pl.program_id(0) == 0)
    def _():
        c_ref[...] = jnp.zeros_like(c_ref)
        n_ref[...] = jnp.zeros_like(n_ref)
        m_ref[...] = jnp.zeros_like(m_ref)

    _mlstm_step(qkv_f_ref, qkv_b_ref, pre_f_ref, pre_b_ref, bias_ref, hf_ref, hb_ref, c_ref, n_ref, m_ref)


def _mlstm(qkv, pre, bias, l, n_ctx):
    rows, width = qkv.shape
    nc = rows // CHUNK
    ncc = n_ctx // CHUNK
    hw = M_HEADS * M_DV
    fwd = lambda t: (t, 0)
    bwd = lambda t: (jnp.where(t < ncc, ncc - 1 - t, ncc + nc - 1 - t), 0)
    blk = 2 * _nbytes((CHUNK, width), BF16) + 2 * _nbytes((CHUNK, LANES), F32) + 2 * _nbytes((CHUNK, hw), F32)
    return pl.pallas_call(
        _mlstm_kernel,
        grid=(nc,),
        in_specs=[pl.BlockSpec((CHUNK, width), fwd), pl.BlockSpec((CHUNK, width), bwd),
                  pl.BlockSpec((CHUNK, LANES), fwd), pl.BlockSpec((CHUNK, LANES), bwd),
                  pl.BlockSpec((None, 1, LANES), lambda t: (l, 0, 0))],
        out_specs=[pl.BlockSpec((CHUNK, hw), fwd), pl.BlockSpec((CHUNK, hw), bwd)],
        out_shape=[jax.ShapeDtypeStruct((rows, hw), F32), jax.ShapeDtypeStruct((rows, hw), F32)],
        scratch_shapes=[pltpu.VMEM(s, F32) for s in SCAN_STATE_SHAPES],
        compiler_params=_params(1, blk, 10 * MIB), name="mlstm_scan",
    )(qkv, qkv, pre, pre, bias)


def _branch_kernel(gu_ref, hf_ref, hb_ref, osig_ref, sgu_g_ref, sgu_w_ref, sgu_bt_ref, mh_g_ref, o_ref, *, a_width):
    gd = a_width // A_GROUPS
    for r0 in range(0, gu_ref.shape[0], CHUNK):
        rs = slice(r0, r0 + CHUNK)
        v = gu_ref[rs, a_width:2 * a_width].astype(F32)
        vc = v - jnp.mean(v, axis=-1, keepdims=True)
        vn = (vc * lax.rsqrt(jnp.mean(vc * vc, axis=-1, keepdims=True) + EPS) * sgu_g_ref[...]).astype(BF16)
        for g in range(A_GROUPS):
            sl = slice(g * gd, (g + 1) * gd)
            s = jnp.dot(sgu_w_ref[g].astype(BF16), vn[:, sl], preferred_element_type=F32) + sgu_bt_ref[:, g:g + 1]
            o_ref[rs, sl] = (gu_ref[rs, sl].astype(F32) * s).astype(o_ref.dtype)
        for hd in range(M_HEADS):
            sl = slice(hd * M_DV, (hd + 1) * M_DV)
            h = hf_ref[rs, sl] + hb_ref[rs, sl]
            hn = h * lax.rsqrt(jnp.mean(h * h, axis=-1, keepdims=True) + EPS)
            o_ref[rs, a_width + hd * M_DV:a_width + (hd + 1) * M_DV] = \
                (hn * mh_g_ref[:, sl] * osig_ref[rs, sl].astype(F32)).astype(o_ref.dtype)


def _branch_inputs(gu, hf, hb, sig, sgu_g, sgu_w, sgu_bt, mh_g, l, a_width):
    n = gu.shape[0]
    hw = M_HEADS * M_DV
    tr = 2 * CHUNK if n % (2 * CHUNK) == 0 else CHUNK
    blk = _nbytes((tr, 2 * a_width), gu.dtype) + 2 * _nbytes((tr, hw), F32) + _nbytes((tr, hw), sig.dtype) \
        + _nbytes((tr, a_width + hw), BF16)
    return pl.pallas_call(
        functools.partial(_branch_kernel, a_width=a_width),
        grid=(n // tr,),
        in_specs=[pl.BlockSpec((tr, 2 * a_width), lambda t: (t, 0)),
                  pl.BlockSpec((tr, hw), lambda t: (t, 0)), pl.BlockSpec((tr, hw), lambda t: (t, 0)),
                  pl.BlockSpec((tr, hw), lambda t: (t, 0)),
                  pl.BlockSpec((None, 1, a_width), lambda t: (l, 0, 0)),
                  pl.BlockSpec((None, A_GROUPS, CHUNK, CHUNK), lambda t: (l, 0, 0, 0)),
                  pl.BlockSpec((None, CHUNK, LANES), lambda t: (l, 0, 0)),
                  pl.BlockSpec((None, 1, hw), lambda t: (l, 0, 0))],
        out_specs=pl.BlockSpec((tr, a_width + hw), lambda t: (t, 0)),
        out_shape=jax.ShapeDtypeStruct((n, a_width + hw), BF16),
        compiler_params=_params(1, blk, 6 * MIB), name="branch_inputs",
    )(gu, hf, hb, sig, sgu_g, sgu_w, sgu_bt, mh_g)


def _final_norm_kernel(x_ref, g_ref, o_ref):
    tm, d = x_ref.shape
    cc = min(COL_CHUNK, d)
    for r0 in range(0, tm, ROW_GROUP):
        inv = _inv_rms(x_ref, r0, ROW_GROUP)
        for c0 in range(0, d, cc):
            cs = slice(c0, c0 + cc)
            o_ref[r0:r0 + ROW_GROUP, cs] = x_ref[r0:r0 + ROW_GROUP, cs] * inv * g_ref[:, cs]


def _final_norm(x, g, n_ctx):
    m, d = x.shape
    tm = _tile(n_ctx, 256, unit=ROW_GROUP)
    skip = n_ctx // tm
    assert skip * tm == n_ctx and (m - n_ctx) % tm == 0
    return pl.pallas_call(
        _final_norm_kernel, grid=((m - n_ctx) // tm,),
        in_specs=[pl.BlockSpec((tm, d), lambda i: (i + skip, 0)), pl.BlockSpec((1, d), lambda i: (0, 0))],
        out_specs=pl.BlockSpec((tm, d), lambda i: (i, 0)),
        out_shape=jax.ShapeDtypeStruct((m - n_ctx, d), F32),
        compiler_params=_params(1, 2 * _nbytes((tm, d), F32), 2 * _nbytes((tm, d), F32)), name="final_norm",
    )(x, g)


def kernel(x, c, ctx, c_ctx, ada_w, ada_b, norm_mix, w_in, sgu_norm, sgu_w, sgu_b, gate_bias, mh_norm, w_br_a, w_br_b, w_out, norm_ffn, dense_w13, dense_w2, moe_router, moe_w13, moe_w2, final_norm):
    bsz, seq, d = x.shape
    n_ctx = ctx.shape[1]
    assert bsz == 1 and seq % CHUNK == 0 and n_ctx % CHUNK == 0
    depth = w_in.shape[0]
    a_width = sgu_norm.shape[1]
    qk_w = M_HEADS * M_DK
    hw = M_HEADS * M_DV
    n_main = 2 * a_width + 2 * qk_w + 2 * hw
    n_gate = 2 * d
    assert mh_norm.shape[1] == hw and a_width == hw and w_in.shape[2] == n_main + IF_WIDTH + n_gate
    o_q, o_o = 2 * a_width, 2 * a_width + 2 * qk_w + hw

    wt_in = jnp.swapaxes(w_in, 1, 2)
    moe_w13_rows = moe_w13.reshape(moe_w13.shape[0], -1, moe_w13.shape[3])
    moe_w2_rows = moe_w2.reshape(moe_w2.shape[0], -1, d)

    def ffn_src(l):
        return (dense_w13, dense_w2, l // 2) if l % 2 == 0 else (moe_w13_rows, moe_w2_rows, l // 2)

    w_if = _cast_w_if(wt_in, n_main)
    router = jnp.pad(moe_router, ((0, 0), (0, 0), (0, LANES - N_EXPERTS))).astype(BF16)
    cond = jnp.zeros((MOD_ROWS, d), F32).at[0].set(c[0]).at[1].set(c_ctx)
    mod = _adaln(cond, ada_w, ada_b)

    q_scale = jnp.concatenate([jnp.full((1, qk_w), M_DK ** -0.5, F32), jnp.ones((1, qk_w + hw), F32)], axis=1)
    bias_if = jnp.pad(gate_bias, ((0, 0), (0, LANES - IF_WIDTH))).reshape(depth, 1, LANES)
    sgu_bt = jnp.pad(jnp.swapaxes(sgu_b, 1, 2), ((0, 0), (0, 0), (0, LANES - A_GROUPS)))
    g_mix, g_ffn = norm_mix.reshape(depth, 1, d), norm_ffn.reshape(depth, 1, d)
    g_sgu, g_mh = sgu_norm.reshape(depth, 1, a_width), mh_norm.reshape(depth, 1, hw)

    w13_src, w2_src, lf = ffn_src(0)
    wts = dict(w_in=_cast_w_in(wt_in, 0, n_main, n_gate), wa=_cast_bf16(w_br_a, 0), wb=_cast_bf16(w_br_b, 0),
               wo=_cast_bf16(w_out, 0), w13=_cast_bf16(w13_src, lf), w2=_cast_bf16(w2_src, lf))

    xs = jnp.concatenate([ctx.reshape(n_ctx, d), x.reshape(seq, d)], axis=0)
    for l in range(depth):
        nxt = {}
        more = l + 1 < depth
        if more:
            w13_src, w2_src, lf = ffn_src(l + 1)

        def ride(kind, *a):
            return (kind, *a) if more else None

        def take(outs, name):
            if more:
                nxt[name] = outs[1]
            return outs[0]

        h = _norm_mod(xs, g_mix, mod, l, 0, n_ctx)
        w_main = wts["w_in"]
        gu = take(_mm_act(h, w_main, 0, 0, o_q, "gelu", BF16, ride=ride("plain", w_br_a, l + 1)), "wa")
        qkv = take(_mm_act(h, w_main, 0, o_q, o_o - o_q, "colscale", BF16, col_scale=q_scale,
                           ride=ride("plain", w_br_b, l + 1)), "wb")
        sig = take(_mm_act(h, w_main, 0, o_o, hw + n_gate, "sigmoid", BF16,
                           ride=ride("w_in", wt_in, l + 1, n_main, n_gate)), "w_in")
        pre = _mm_act(h, w_if, l, 0, LANES, "none", F32)[0]
        hf, hb = _mlstm(qkv, pre, bias_if, l, n_ctx)
        t = _branch_inputs(gu, hf, hb, sig, g_sgu, sgu_w, sgu_bt, g_mh, l, a_width)
        z = _mm_merge(t, wts["wa"], wts["wb"], sig)
        xs = take(_mm_residual(z, wts["wo"], xs, mod, l, 2, n_ctx, ride=ride("plain", w_out, l + 1)), "wo")
        if l % 2 == 0:
            h2 = _norm_mod(xs, g_ffn, mod, l, 3, n_ctx)
            act = take(_ffn_up_dense(h2, wts["w13"], ride=ride("plain", w13_src, lf)), "w13")
        else:
            h2, gates = _norm_mod(xs, g_ffn, mod, l, 3, n_ctx, router=router[l // 2])
            w13 = wts["w13"].reshape(N_EXPERTS, d, -1)
            act = take(_ffn_up_moe(h2, w13, gates, ride=ride("plain", w13_src, lf)), "w13")
        xs = take(_mm_residual(act, wts["w2"], xs, mod, l, 5, n_ctx, ride=ride("plain", w2_src, lf)), "w2")
        wts = nxt
    return _final_norm(xs, final_norm.reshape(1, d), n_ctx).reshape(bsz, seq, d)
```

```python
import functools
import math

import jax
import jax.numpy as jnp
from jax import lax
from jax.experimental import pallas as pl
from jax.experimental.pallas import tpu as pltpu

F32 = jnp.float32
BF16 = jnp.bfloat16

CHUNK = 128
A_GROUPS = 4
M_HEADS = 4
M_DK = 128
M_DV = 256
N_DIRS = 2
IF_WIDTH = N_DIRS * 2 * M_HEADS
GATE_SOFTCAP = 15.0
N_EXPERTS = 8
N_MOD = 6
EPS = 1e-6
MOD_ROWS = 8

LANES = 128
BF16_SUBLANES = 16
V7X_VMEM_BYTES = 64 * 1024 * 1024
VMEM_LIMIT_CAP = V7X_VMEM_BYTES - 3 * 1024 * 1024
MIB = 1024 * 1024


def _params(n_grid_dims, block_bytes, temp_bytes=0):
    need = 2 * block_bytes + temp_bytes + 8 * MIB
    return pltpu.CompilerParams(
        dimension_semantics=("arbitrary",) * n_grid_dims,
        vmem_limit_bytes=int(min(max(need, 16 * MIB), VMEM_LIMIT_CAP)))


def _nbytes(shape, dtype):
    return math.prod(shape) * jnp.dtype(dtype).itemsize


def _tile(n, pref, unit=LANES):
    if n <= pref:
        return n
    t = pref - pref % unit
    while n % t:
        t -= unit
    assert t > 0
    return t


def _sigmoid(x):
    return 1.0 / (1.0 + jnp.exp(-x))


def _sigmoid_tanh(x):
    return 0.5 + 0.5 * jnp.tanh(0.5 * x)


def _gelu(x):
    return 0.5 * x * (1.0 + jnp.tanh(0.7978845608028654 * (x + 0.044715 * (x * x * x))))


def _log_sigmoid(x):
    return jnp.minimum(x, 0.0) - jnp.log1p(jnp.exp(-jnp.abs(x)))


def _row_select(mod_ref, tile_rows, tile_index, n_ctx):
    rows = tile_index * tile_rows + lax.broadcasted_iota(jnp.int32, (tile_rows, 1), 0)
    return jnp.where(rows < n_ctx, mod_ref[1:2, :], mod_ref[0:1, :])


def _wspec(w, l, block, index):
    if w.ndim == 2:
        return pl.BlockSpec(block, index)
    return pl.BlockSpec((None,) + block, lambda *g: (l,) + index(*g))


def _cast_kernel(w_ref, o_ref):
    o_ref[...] = w_ref[...].astype(o_ref.dtype)


def _cast_bf16(w, l):
    _, r, c = w.shape
    tr = _tile(r, max(BF16_SUBLANES, (8 * MIB) // (4 * c)), unit=BF16_SUBLANES)
    return pl.pallas_call(
        _cast_kernel, grid=(r // tr,),
        in_specs=[pl.BlockSpec((None, tr, c), lambda i: (l, i, 0))],
        out_specs=pl.BlockSpec((tr, c), lambda i: (i, 0)),
        out_shape=jax.ShapeDtypeStruct((r, c), BF16),
        compiler_params=_params(1, _nbytes((tr, c), F32) + _nbytes((tr, c), BF16)), name="cast_bf16",
    )(w)


def _cast_w_if_kernel(w_ref, o_ref):
    o_ref[...] = jnp.zeros_like(o_ref)
    o_ref[:IF_WIDTH, :] = w_ref[...].astype(BF16)


def _cast_w_if(wt, n_main):
    n_l, _, d = wt.shape
    assert n_main % IF_WIDTH == 0
    return pl.pallas_call(
        _cast_w_if_kernel, grid=(n_l,),
        in_specs=[pl.BlockSpec((None, IF_WIDTH, d), lambda l: (l, n_main // IF_WIDTH, 0))],
        out_specs=pl.BlockSpec((None, LANES, d), lambda l: (l, 0, 0)),
        out_shape=jax.ShapeDtypeStruct((n_l, LANES, d), BF16),
        compiler_params=_params(1, _nbytes((LANES, d), F32)), name="cast_w_if",
    )(wt)


def _w_in_rows(a_ref, b_ref, o_ref, is_gate):
    tr = a_ref.shape[0]
    keep = tr - IF_WIDTH
    o_ref[:keep, :] = jnp.where(is_gate, a_ref[IF_WIDTH:, :], a_ref[:keep, :]).astype(BF16)
    o_ref[keep:, :] = jnp.where(is_gate, b_ref[...], a_ref[keep:, :]).astype(BF16)


def _w_in_specs(wt, l, n_main, n_gate, tr, chunk_of):
    _, n, d = wt.shape
    assert n == n_main + IF_WIDTH + n_gate and n_main % tr == 0 and n_gate % tr == 0 and tr % IF_WIDTH == 0
    per = tr // IF_WIDTH
    ins = [pl.BlockSpec((None, tr, d), lambda *g: (l, chunk_of(*g), 0)),
           pl.BlockSpec((None, IF_WIDTH, d), lambda *g: (l, per * (chunk_of(*g) + 1), 0))]
    out = pl.BlockSpec((tr, d), lambda *g: (chunk_of(*g), 0))
    return ins, out, jax.ShapeDtypeStruct((n_main + n_gate, d), BF16)


def _cast_w_in_kernel(a_ref, b_ref, o_ref, *, main_chunks):
    _w_in_rows(a_ref, b_ref, o_ref, pl.program_id(0) >= main_chunks)


def _cast_w_in(wt, l, n_main, n_gate):
    tr = 512
    ins, out, shape = _w_in_specs(wt, l, n_main, n_gate, tr, lambda i: i)
    d = wt.shape[2]
    return pl.pallas_call(
        functools.partial(_cast_w_in_kernel, main_chunks=n_main // tr), grid=((n_main + n_gate) // tr,),
        in_specs=ins, out_specs=out, out_shape=shape,
        compiler_params=_params(1, _nbytes((tr, d), F32) + _nbytes((tr, d), BF16)), name="cast_w_in",
    )(wt, wt)


def _ride_rows(r, n_steps):
    for cr in range(BF16_SUBLANES, r + 1, BF16_SUBLANES):
        if r % cr == 0 and r // cr <= n_steps:
            return cr
    return None


def _pallas(kern, grid, in_specs, args, out_specs, out_shapes, blk, tmp, name, ride=None):
    n_out = len(out_specs)

    def call(k, ins, outs, shapes, a, extra_blk=0):
        return pl.pallas_call(k, grid=grid, in_specs=ins, out_specs=outs, out_shape=shapes,
                              compiler_params=_params(len(grid), blk + extra_blk, tmp), name=name)(*a)

    if ride is None:
        return call(kern, in_specs, out_specs, out_shapes, args)
    n_steps = math.prod(grid)
    strides = [math.prod(grid[d + 1:]) for d in range(len(grid))]
    n_in = len(in_specs)

    def step(*g):
        return sum(gi * st for gi, st in zip(g, strides))

    if ride[0] == "plain":
        _, src, ls = ride
        _, r, c = src.shape
        cr = _ride_rows(r, n_steps)
        if cr is None:
            return (*call(kern, in_specs, out_specs, out_shapes, args), _cast_bf16(src, ls))
        last = r // cr - 1

        def chunk(*g):
            return jnp.minimum(step(*g), last)

        def riding(*refs):
            src_ref, dst_ref = refs[n_in], refs[n_in + 1 + n_out]
            dst_ref[...] = src_ref[...].astype(BF16)
            kern(*refs[:n_in], *refs[n_in + 1:n_in + 1 + n_out], *refs[n_in + 2 + n_out:])

        return call(riding, in_specs + [pl.BlockSpec((None, cr, c), lambda *g: (ls, chunk(*g), 0))],
                    out_specs + [pl.BlockSpec((cr, c), lambda *g: (chunk(*g), 0))],
                    out_shapes + [jax.ShapeDtypeStruct((r, c), BF16)], args + [src],
                    _nbytes((cr, c), F32) + _nbytes((cr, c), BF16))

    _, wt, ls, n_main, n_gate = ride
    d = wt.shape[2]
    tr = next((t for t in range(IF_WIDTH, n_main + 1, IF_WIDTH)
               if n_main % t == 0 and n_gate % t == 0 and (n_main + n_gate) // t <= n_steps), None)
    if tr is None:
        return (*call(kern, in_specs, out_specs, out_shapes, args), _cast_w_in(wt, ls, n_main, n_gate))
    last = (n_main + n_gate) // tr - 1
    main_chunks = n_main // tr

    def chunk(*g):
        return jnp.minimum(step(*g), last)

    ins, out, shape = _w_in_specs(wt, ls, n_main, n_gate, tr, chunk)

    def riding(*refs):
        a_ref, b_ref, dst_ref = refs[n_in], refs[n_in + 1], refs[n_in + 2 + n_out]
        ids = [pl.program_id(dd) for dd in range(len(grid))]
        _w_in_rows(a_ref, b_ref, dst_ref, chunk(*ids) >= main_chunks)
        kern(*refs[:n_in], *refs[n_in + 2:n_in + 2 + n_out], *refs[n_in + 3 + n_out:])

    return call(riding, in_specs + ins, out_specs + [out], out_shapes + [shape], args + [wt, wt],
                _nbytes((tr, d), F32) + _nbytes((tr, d), BF16))


def _adaln_kernel(cond_ref, w_ref, b_ref, o_ref):
    c = cond_ref[...]
    s = (c * _sigmoid(c)).astype(BF16)
    acc = jnp.dot(s, w_ref[...].astype(BF16), preferred_element_type=F32)
    o_ref[...] = acc + b_ref[...]


def _adaln(cond, ada_w, ada_b):
    n_layers, d, n = ada_w.shape
    r = cond.shape[0]
    tn = _tile(n, 512)
    blk = _nbytes((d, tn), F32) + _nbytes((r, d), F32) + 2 * _nbytes((r, tn), F32)
    return pl.pallas_call(
        _adaln_kernel,
        grid=(n_layers, n // tn),
        in_specs=[pl.BlockSpec((r, d), lambda l, j: (0, 0)),
                  pl.BlockSpec((None, d, tn), lambda l, j: (l, 0, j)),
                  pl.BlockSpec((None, 1, tn), lambda l, j: (l, 0, j))],
        out_specs=pl.BlockSpec((None, r, tn), lambda l, j: (l, 0, j)),
        out_shape=jax.ShapeDtypeStruct((n_layers, r, n), F32),
        compiler_params=_params(2, blk, _nbytes((d, tn), BF16)), name="adaln",
    )(cond, ada_w, ada_b.reshape(n_layers, 1, n))


ROW_GROUP = 16
COL_CHUNK = 1024


def _inv_rms(x_ref, r0, rows):
    d = x_ref.shape[1]
    cc = min(COL_CHUNK, d)
    ss = None
    for c0 in range(0, d, cc):
        xc = x_ref[r0:r0 + rows, c0:c0 + cc]
        part = jnp.sum(xc * xc, axis=-1, keepdims=True)
        ss = part if ss is None else ss + part
    return lax.rsqrt(ss / d + EPS)


def _norm_mod_kernel(x_ref, g_ref, shift_ref, scale_ref, *rest, n_ctx, routed):
    tm, d = x_ref.shape
    cc = min(COL_CHUNK, d)
    o_ref = rest[1] if routed else rest[0]
    row0 = pl.program_id(0) * tm

    def normalise(row_of_group):
        for r0 in range(0, tm, ROW_GROUP):
            mr = row_of_group(r0)
            inv = _inv_rms(x_ref, r0, ROW_GROUP)
            for c0 in range(0, d, cc):
                cs = slice(c0, c0 + cc)
                y = x_ref[r0:r0 + ROW_GROUP, cs] * inv * g_ref[:, cs]
                o_ref[r0:r0 + ROW_GROUP, cs] = (y * (1.0 + scale_ref[pl.ds(mr, 1), cs])
                                                + shift_ref[pl.ds(mr, 1), cs]).astype(BF16)

    @pl.when(row0 >= n_ctx)
    def _():
        normalise(lambda r0: 0)

    @pl.when(row0 < n_ctx)
    def _():
        normalise(lambda r0: jnp.where(row0 + r0 < n_ctx, 1, 0))
    if not routed:
        return
    r_ref, _, gate_ref = rest
    logits = jnp.dot(o_ref[...], r_ref[...], preferred_element_type=F32)
    lane = lax.broadcasted_iota(jnp.int32, logits.shape, 1)
    logits = jnp.where(lane < N_EXPERTS, logits, -jnp.inf)
    m1 = jnp.max(logits, axis=-1, keepdims=True)
    i1 = jnp.min(jnp.where(logits == m1, lane, LANES), axis=-1, keepdims=True)
    rest_l = jnp.where(lane == i1, -jnp.inf, logits)
    m2 = jnp.max(rest_l, axis=-1, keepdims=True)
    i2 = jnp.min(jnp.where(rest_l == m2, lane, LANES), axis=-1, keepdims=True)
    e2 = jnp.exp(m2 - m1)
    gate_ref[...] = jnp.where(lane == i1, 1.0 / (1.0 + e2), 0.0) + jnp.where(lane == i2, e2 / (1.0 + e2), 0.0)


def _norm_mod(x, g, mod, l, k_shift, n_ctx, router=None):
    m, d = x.shape
    tm = _tile(m, 640, unit=ROW_GROUP)
    assert n_ctx % ROW_GROUP == 0 and m % tm == 0
    in_specs = [pl.BlockSpec((tm, d), lambda i: (i, 0)),
                pl.BlockSpec((None, 1, d), lambda i: (l, 0, 0)),
                pl.BlockSpec((None, MOD_ROWS, d), lambda i: (l, 0, k_shift)),
                pl.BlockSpec((None, MOD_ROWS, d), lambda i: (l, 0, k_shift + 1))]
    blk = _nbytes((tm, d), F32) + _nbytes((tm, d), BF16)
    tmp = 3 * _nbytes((tm, d), F32)
    kern = functools.partial(_norm_mod_kernel, n_ctx=n_ctx, routed=router is not None)
    if router is None:
        return pl.pallas_call(
            kern, grid=(m // tm,), in_specs=in_specs,
            out_specs=pl.BlockSpec((tm, d), lambda i: (i, 0)),
            out_shape=jax.ShapeDtypeStruct((m, d), BF16),
            compiler_params=_params(1, blk, tmp), name="norm_mod",
        )(x, g, mod, mod)
    return pl.pallas_call(
        kern, grid=(m // tm,),
        in_specs=in_specs + [pl.BlockSpec((d, LANES), lambda i: (0, 0))],
        out_specs=[pl.BlockSpec((tm, d), lambda i: (i, 0)), pl.BlockSpec((tm, LANES), lambda i: (i, 0))],
        out_shape=[jax.ShapeDtypeStruct((m, d), BF16), jax.ShapeDtypeStruct((m, LANES), F32)],
        compiler_params=_params(1, blk + _nbytes((d, LANES), BF16), tmp), name="norm_mod_router",
    )(x, g, mod, mod, router)


def _mm_act_kernel(x_ref, wt_ref, *rest, act):
    acc = lax.dot_general(x_ref[...], wt_ref[...], (((1,), (1,)), ((), ())), preferred_element_type=F32)
    if act == "gelu":
        acc = _gelu(acc)
    elif act == "sigmoid":
        acc = _sigmoid_tanh(acc)
    elif act == "colscale":
        acc = acc * rest[0][...]
    rest[-1][...] = acc.astype(rest[-1].dtype)


def _mm_act(x, wt, l, row0, n, act, out_dtype, col_scale=None, ride=None, tm_pref=1280, tn_pref=1024):
    m, k = x.shape
    tm, tn = _tile(m, tm_pref), _tile(n, tn_pref)
    assert row0 % tn == 0
    rb = row0 // tn
    in_specs = [pl.BlockSpec((tm, k), lambda i, j: (i, 0)), _wspec(wt, l, (tn, k), lambda i, j: (rb + j, 0))]
    args = [x, wt]
    if act == "colscale":
        in_specs.append(pl.BlockSpec((1, tn), lambda i, j: (0, j)))
        args.append(col_scale)
    blk = _nbytes((tm, k), BF16) + _nbytes((tn, k), BF16) + _nbytes((tm, tn), out_dtype)
    return _pallas(functools.partial(_mm_act_kernel, act=act), (m // tm, n // tn), in_specs, args,
                   [pl.BlockSpec((tm, tn), lambda i, j: (i, j))], [jax.ShapeDtypeStruct((m, n), out_dtype)],
                   blk, 2 * _nbytes((tm, tn), F32), "mm_" + act, ride)


def _mm_residual_kernel(z_ref, w_ref, x_ref, g_ref, o_ref, *, n_ctx):
    acc = jnp.dot(z_ref[...], w_ref[...], preferred_element_type=F32)
    gate = _row_select(g_ref, z_ref.shape[0], pl.program_id(0), n_ctx)
    o_ref[...] = x_ref[...] + gate * acc


def _mm_residual(z, w, x, mod, l_mod, k_gate, n_ctx, ride=None, tm_pref=1280, tn_pref=512):
    m, k = z.shape
    n = w.shape[1]
    tm, tn = _tile(m, tm_pref), _tile(n, tn_pref)
    gb = k_gate * (n // tn)
    blk = _nbytes((tm, k), BF16) + _nbytes((k, tn), BF16) + 2 * _nbytes((tm, tn), F32)
    in_specs = [pl.BlockSpec((tm, k), lambda i, j: (i, 0)), pl.BlockSpec((k, tn), lambda i, j: (0, j)),
                pl.BlockSpec((tm, tn), lambda i, j: (i, j)),
                pl.BlockSpec((None, MOD_ROWS, tn), lambda i, j: (l_mod, 0, gb + j))]
    return _pallas(functools.partial(_mm_residual_kernel, n_ctx=n_ctx), (m // tm, n // tn), in_specs, [z, w, x, mod],
                   [pl.BlockSpec((tm, tn), lambda i, j: (i, j))], [jax.ShapeDtypeStruct((m, n), F32)],
                   blk, 2 * _nbytes((tm, tn), F32), "mm_residual", ride)


def _mm_merge_kernel(ta_ref, tb_ref, wa_ref, wb_ref, ga_ref, gb_ref, o_ref):
    ya = jnp.dot(ta_ref[...], wa_ref[...], preferred_element_type=F32)
    yb = jnp.dot(tb_ref[...], wb_ref[...], preferred_element_type=F32)
    o_ref[...] = (ga_ref[...].astype(F32) * ya + gb_ref[...].astype(F32) * yb).astype(o_ref.dtype)


def _mm_merge(t, wa, wb, sig, tm_pref=1280, tn_pref=1024):
    m = t.shape[0]
    ka, n = wa.shape
    assert wb.shape == wa.shape and t.shape[1] == 2 * ka
    ko = sig.shape[1] - 2 * n
    tm, tn = _tile(m, tm_pref), _tile(n, tn_pref)
    off_a = ko // tn
    off_b = (ko + n) // tn
    assert off_a * tn == ko and off_b * tn == ko + n
    blk = 2 * _nbytes((tm, ka), BF16) + 2 * _nbytes((ka, tn), BF16) + 2 * _nbytes((tm, tn), sig.dtype) \
        + _nbytes((tm, tn), BF16)
    return pl.pallas_call(
        _mm_merge_kernel,
        grid=(m // tm, n // tn),
        in_specs=[pl.BlockSpec((tm, ka), lambda i, j: (i, 0)), pl.BlockSpec((tm, ka), lambda i, j: (i, 1)),
                  pl.BlockSpec((ka, tn), lambda i, j: (0, j)), pl.BlockSpec((ka, tn), lambda i, j: (0, j)),
                  pl.BlockSpec((tm, tn), lambda i, j: (i, j + off_a)),
                  pl.BlockSpec((tm, tn), lambda i, j: (i, j + off_b))],
        out_specs=pl.BlockSpec((tm, tn), lambda i, j: (i, j)),
        out_shape=jax.ShapeDtypeStruct((m, n), BF16),
        compiler_params=_params(2, blk, 3 * _nbytes((tm, tn), F32)), name="mm_merge",
    )(t, t, wa, wb, sig, sig)


def _ffn_up_kernel(h_ref, wg_ref, wu_ref, *rest, routed):
    h = h_ref[...]
    g = jnp.dot(h, wg_ref[...], preferred_element_type=F32)
    u = jnp.dot(h, wu_ref[...], preferred_element_type=F32)
    act = g * _sigmoid_tanh(g) * u
    if routed:
        gates = rest[0][...]
        lane = lax.broadcasted_iota(jnp.int32, gates.shape, 1)
        act = act * jnp.sum(jnp.where(lane == pl.program_id(1), gates, 0.0), axis=-1, keepdims=True)
    rest[-1][...] = act.astype(rest[-1].dtype)


def _ffn_up_dense(h, w13, ride=None, tm_pref=1280, tn_pref=512):
    m, k = h.shape
    f = w13.shape[1] // 2
    tm, tn = _tile(m, tm_pref), _tile(f, tn_pref)
    nb = f // tn
    blk = _nbytes((tm, k), BF16) + 2 * _nbytes((k, tn), BF16) + _nbytes((tm, tn), BF16)
    in_specs = [pl.BlockSpec((tm, k), lambda i, j: (i, 0)), pl.BlockSpec((k, tn), lambda i, j: (0, j)),
                pl.BlockSpec((k, tn), lambda i, j: (0, j + nb))]
    return _pallas(functools.partial(_ffn_up_kernel, routed=False), (m // tm, nb), in_specs, [h, w13, w13],
                   [pl.BlockSpec((tm, tn), lambda i, j: (i, j))], [jax.ShapeDtypeStruct((m, f), BF16)],
                   blk, 3 * _nbytes((tm, tn), F32), "ffn_up_dense", ride)


def _ffn_up_moe(h, w13, gates, ride=None, tm_pref=1280):
    m, k = h.shape
    two_de = w13.shape[2]
    de = two_de // 2
    tm = _tile(m, tm_pref)
    blk = _nbytes((tm, k), BF16) + 2 * _nbytes((k, de), BF16) + _nbytes((tm, de), BF16) + _nbytes((tm, LANES), F32)
    in_specs = [pl.BlockSpec((tm, k), lambda i, j: (i, 0)), pl.BlockSpec((None, k, de), lambda i, j: (j, 0, 0)),
                pl.BlockSpec((None, k, de), lambda i, j: (j, 0, 1)), pl.BlockSpec((tm, LANES), lambda i, j: (i, 0))]
    return _pallas(functools.partial(_ffn_up_kernel, routed=True), (m // tm, N_EXPERTS), in_specs, [h, w13, w13, gates],
                   [pl.BlockSpec((tm, de), lambda i, j: (i, j))], [jax.ShapeDtypeStruct((m, N_EXPERTS * de), BF16)],
                   blk, 3 * _nbytes((tm, de), F32), "ffn_up_moe", ride)


def _split3(x):
    hi = x.astype(BF16)
    r1 = x - hi.astype(F32)
    mid = r1.astype(BF16)
    lo = (r1 - mid.astype(F32)).astype(BF16)
    return hi, mid, lo


def _mlstm_step(qkv_f_ref, qkv_b_ref, pre_f_ref, pre_b_ref, bias_ref, hf_ref, hb_ref, c_ref, n_ref, m_ref, offs):
    rows = lax.broadcasted_iota(jnp.int32, (CHUNK, CHUNK), 0)
    cols = lax.broadcasted_iota(jnp.int32, (CHUNK, CHUNK), 1)
    for d, (qkv_ref, pre_ref, out_ref) in enumerate(((qkv_f_ref, pre_f_ref, hf_ref), (qkv_b_ref, pre_b_ref, hb_ref))):
        rs = slice(offs[d], offs[d] + CHUNK)
        mask = (rows >= cols) if d == 0 else (rows <= cols)
        tri = jnp.where(mask, 1.0, 0.0).astype(BF16)
        a = pre_ref[rs, :] + bias_ref[...]
        a = GATE_SOFTCAP * jnp.tanh(a / GATE_SOFTCAP)
        lf = _log_sigmoid(a)
        hi, mid, lo = _split3(lf)
        b_all = (jnp.dot(tri, hi, preferred_element_type=F32) + jnp.dot(tri, mid, preferred_element_type=F32)
                 + jnp.dot(tri, lo, preferred_element_type=F32))
        a_t = a.T
        b_t = b_all.T
        last = CHUNK - 1 if d == 0 else 0
        for hd in range(M_HEADS):
            ci = d * 2 * M_HEADS + hd
            cf = ci + M_HEADS
            r = d * M_HEADS + hd
            ig_col, ig_row = a[:, ci:ci + 1], a_t[ci:ci + 1, :]
            b_col, b_row = b_all[:, cf:cf + 1], b_t[cf:cf + 1, :]
            b_last = b_col[last:last + 1, :]
            m_prev = m_ref[r:r + 1, 0:1]
            c_prev = c_ref[r]
            n_prev = n_ref[r:r + 1, :]
            q = qkv_ref[rs, hd * M_DK:(hd + 1) * M_DK]
            k = qkv_ref[rs, M_HEADS * M_DK + hd * M_DK:M_HEADS * M_DK + (hd + 1) * M_DK]
            v = qkv_ref[rs, 2 * M_HEADS * M_DK + hd * M_DV:2 * M_HEADS * M_DK + (hd + 1) * M_DV]

            log_d = jnp.where(mask, b_col - b_row + ig_row, -jnp.inf)
            m_inter = b_col + m_prev
            m_row = jnp.maximum(m_inter, jnp.max(log_d, axis=-1, keepdims=True))
            qk = lax.dot_general(q, k, (((1,), (1,)), ((), ())), preferred_element_type=F32)
            s = qk * jnp.exp(log_d - m_row)
            w_inter = jnp.exp(m_inter - m_row)
            qc = jnp.dot(q, c_prev.astype(BF16), preferred_element_type=F32)
            num = jnp.dot(s.astype(BF16), v, preferred_element_type=F32) + w_inter * qc
            qn = jnp.sum(q.astype(F32) * n_prev, axis=-1, keepdims=True)
            den = jnp.sum(s, axis=-1, keepdims=True) + w_inter * qn
            out_ref[rs, hd * M_DV:(hd + 1) * M_DV] = num / jnp.maximum(jnp.abs(den), jnp.exp(-m_row))

            log_w = b_last - b_col + ig_col
            m_new = jnp.maximum(b_last + m_prev, jnp.max(log_w, axis=0, keepdims=True))
            w = jnp.exp(log_w - m_new)
            decay = jnp.exp(b_last + m_prev - m_new)
            wk = w * k.astype(F32)
            c_ref[r] = decay * c_prev + lax.dot_general(wk.astype(BF16), v, (((0,), (0,)), ((), ())),
                                                        preferred_element_type=F32)
            n_ref[r:r + 1, :] = decay * n_prev + jnp.sum(wk, axis=0, keepdims=True)
            m_ref[r:r + 1, :] = jnp.broadcast_to(m_new, (1, LANES))


SCAN_STATE_SHAPES = ((N_DIRS * M_HEADS, M_DK, M_DV), (N_DIRS * M_HEADS, M_DK), (N_DIRS * M_HEADS, LANES))


def _mlstm_kernel(qkv_f_ref, qkv_b_ref, pre_f_ref, pre_b_ref, bias_ref, hf_ref, hb_ref, c_ref, n_ref, m_ref):
    @pl.when(pl.program_id(0) == 0)
    def _():
        c_ref[...] = jnp.zeros_like(c_ref)
        n_ref[...] = jnp.zeros_like(n_ref)
        m_ref[...] = jnp.zeros_like(m_ref)

    per_step = qkv_f_ref.shape[0] // CHUNK
    for p in range(per_step):
        _mlstm_step(qkv_f_ref, qkv_b_ref, pre_f_ref, pre_b_ref, bias_ref, hf_ref, hb_ref, c_ref, n_ref, m_ref,
                    offs=(p * CHUNK, (per_step - 1 - p) * CHUNK))


def _mlstm(qkv, pre, bias, l, n_ctx):
    rows, width = qkv.shape
    per_step = 2 if (rows // CHUNK) % 2 == 0 and (n_ctx // CHUNK) % 2 == 0 else 1
    tr = per_step * CHUNK
    nc = rows // tr
    ncc = n_ctx // tr
    hw = M_HEADS * M_DV
    fwd = lambda t: (t, 0)
    bwd = lambda t: (jnp.where(t < ncc, ncc - 1 - t, ncc + nc - 1 - t), 0)
    blk = 2 * _nbytes((tr, width), BF16) + 2 * _nbytes((tr, LANES), F32) + 2 * _nbytes((tr, hw), F32)
    return pl.pallas_call(
        _mlstm_kernel,
        grid=(nc,),
        in_specs=[pl.BlockSpec((tr, width), fwd), pl.BlockSpec((tr, width), bwd),
                  pl.BlockSpec((tr, LANES), fwd), pl.BlockSpec((tr, LANES), bwd),
                  pl.BlockSpec((None, 1, LANES), lambda t: (l, 0, 0))],
        out_specs=[pl.BlockSpec((tr, hw), fwd), pl.BlockSpec((tr, hw), bwd)],
        out_shape=[jax.ShapeDtypeStruct((rows, hw), F32), jax.ShapeDtypeStruct((rows, hw), F32)],
        scratch_shapes=[pltpu.VMEM(s, F32) for s in SCAN_STATE_SHAPES],
        compiler_params=_params(1, blk, 10 * MIB), name="mlstm_scan",
    )(qkv, qkv, pre, pre, bias)


def _branch_kernel(gu_ref, hf_ref, hb_ref, osig_ref, sgu_g_ref, sgu_w_ref, sgu_bt_ref, mh_g_ref, o_ref, *, a_width):
    gd = a_width // A_GROUPS
    for r0 in range(0, gu_ref.shape[0], CHUNK):
        rs = slice(r0, r0 + CHUNK)
        v = gu_ref[rs, a_width:2 * a_width].astype(F32)
        vc = v - jnp.mean(v, axis=-1, keepdims=True)
        vn = (vc * lax.rsqrt(jnp.mean(vc * vc, axis=-1, keepdims=True) + EPS) * sgu_g_ref[...]).astype(BF16)
        for g in range(A_GROUPS):
            sl = slice(g * gd, (g + 1) * gd)
            s = jnp.dot(sgu_w_ref[g].astype(BF16), vn[:, sl], preferred_element_type=F32) + sgu_bt_ref[:, g:g + 1]
            o_ref[rs, sl] = (gu_ref[rs, sl].astype(F32) * s).astype(o_ref.dtype)
        for hd in range(M_HEADS):
            sl = slice(hd * M_DV, (hd + 1) * M_DV)
            h = hf_ref[rs, sl] + hb_ref[rs, sl]
            hn = h * lax.rsqrt(jnp.mean(h * h, axis=-1, keepdims=True) + EPS)
            o_ref[rs, a_width + hd * M_DV:a_width + (hd + 1) * M_DV] = \
                (hn * mh_g_ref[:, sl] * osig_ref[rs, sl].astype(F32)).astype(o_ref.dtype)


def _branch_inputs(gu, hf, hb, sig, sgu_g, sgu_w, sgu_bt, mh_g, l, a_width):
    n = gu.shape[0]
    hw = M_HEADS * M_DV
    tr = _tile(n, 640, unit=CHUNK)
    blk = _nbytes((tr, 2 * a_width), gu.dtype) + 2 * _nbytes((tr, hw), F32) + _nbytes((tr, hw), sig.dtype) \
        + _nbytes((tr, a_width + hw), BF16)
    return pl.pallas_call(
        functools.partial(_branch_kernel, a_width=a_width),
        grid=(n // tr,),
        in_specs=[pl.BlockSpec((tr, 2 * a_width), lambda t: (t, 0)),
                  pl.BlockSpec((tr, hw), lambda t: (t, 0)), pl.BlockSpec((tr, hw), lambda t: (t, 0)),
                  pl.BlockSpec((tr, hw), lambda t: (t, 0)),
                  pl.BlockSpec((None, 1, a_width), lambda t: (l, 0, 0)),
                  pl.BlockSpec((None, A_GROUPS, CHUNK, CHUNK), lambda t: (l, 0, 0, 0)),
                  pl.BlockSpec((None, CHUNK, LANES), lambda t: (l, 0, 0)),
                  pl.BlockSpec((None, 1, hw), lambda t: (l, 0, 0))],
        out_specs=pl.BlockSpec((tr, a_width + hw), lambda t: (t, 0)),
        out_shape=jax.ShapeDtypeStruct((n, a_width + hw), BF16),
        compiler_params=_params(1, blk, 6 * MIB), name="branch_inputs",
    )(gu, hf, hb, sig, sgu_g, sgu_w, sgu_bt, mh_g)


def _final_norm_kernel(x_ref, g_ref, o_ref):
    tm, d = x_ref.shape
    cc = min(COL_CHUNK, d)
    for r0 in range(0, tm, ROW_GROUP):
        inv = _inv_rms(x_ref, r0, ROW_GROUP)
        for c0 in range(0, d, cc):
            cs = slice(c0, c0 + cc)
            o_ref[r0:r0 + ROW_GROUP, cs] = x_ref[r0:r0 + ROW_GROUP, cs] * inv * g_ref[:, cs]


def _final_norm(x, g, n_ctx):
    m, d = x.shape
    tm = _tile(n_ctx, 256, unit=ROW_GROUP)
    skip = n_ctx // tm
    assert skip * tm == n_ctx and (m - n_ctx) % tm == 0
    return pl.pallas_call(
        _final_norm_kernel, grid=((m - n_ctx) // tm,),
        in_specs=[pl.BlockSpec((tm, d), lambda i: (i + skip, 0)), pl.BlockSpec((1, d), lambda i: (0, 0))],
        out_specs=pl.BlockSpec((tm, d), lambda i: (i, 0)),
        out_shape=jax.ShapeDtypeStruct((m - n_ctx, d), F32),
        compiler_params=_params(1, 2 * _nbytes((tm, d), F32), 2 * _nbytes((tm, d), F32)), name="final_norm",
    )(x, g)


def kernel(x, c, ctx, c_ctx, ada_w, ada_b, norm_mix, w_in, sgu_norm, sgu_w, sgu_b, gate_bias, mh_norm, w_br_a, w_br_b, w_out, norm_ffn, dense_w13, dense_w2, moe_router, moe_w13, moe_w2, final_norm):
    bsz, seq, d = x.shape
    n_ctx = ctx.shape[1]
    assert bsz == 1 and seq % CHUNK == 0 and n_ctx % CHUNK == 0
    depth = w_in.shape[0]
    a_width = sgu_norm.shape[1]
    qk_w = M_HEADS * M_DK
    hw = M_HEADS * M_DV
    n_main = 2 * a_width + 2 * qk_w + 2 * hw
    n_gate = 2 * d
    assert mh_norm.shape[1] == hw and a_width == hw and w_in.shape[2] == n_main + IF_WIDTH + n_gate
    o_q, o_o = 2 * a_width, 2 * a_width + 2 * qk_w + hw

    wt_in = jnp.swapaxes(w_in, 1, 2)
    moe_w13_rows = moe_w13.reshape(moe_w13.shape[0], -1, moe_w13.shape[3])
    moe_w2_rows = moe_w2.reshape(moe_w2.shape[0], -1, d)

    def ffn_src(l):
        return (dense_w13, dense_w2, l // 2) if l % 2 == 0 else (moe_w13_rows, moe_w2_rows, l // 2)

    w_if = _cast_w_if(wt_in, n_main)
    router = jnp.pad(moe_router, ((0, 0), (0, 0), (0, LANES - N_EXPERTS))).astype(BF16)
    cond = jnp.zeros((MOD_ROWS, d), F32).at[0].set(c[0]).at[1].set(c_ctx)
    mod = _adaln(cond, ada_w, ada_b)

    q_scale = jnp.concatenate([jnp.full((1, qk_w), M_DK ** -0.5, F32), jnp.ones((1, qk_w + hw), F32)], axis=1)
    bias_if = jnp.pad(gate_bias, ((0, 0), (0, LANES - IF_WIDTH))).reshape(depth, 1, LANES)
    sgu_bt = jnp.pad(jnp.swapaxes(sgu_b, 1, 2), ((0, 0), (0, 0), (0, LANES - A_GROUPS)))
    g_mix, g_ffn = norm_mix.reshape(depth, 1, d), norm_ffn.reshape(depth, 1, d)
    g_sgu, g_mh = sgu_norm.reshape(depth, 1, a_width), mh_norm.reshape(depth, 1, hw)

    w13_src, w2_src, lf = ffn_src(0)
    wts = dict(w_in=_cast_w_in(wt_in, 0, n_main, n_gate), wa=_cast_bf16(w_br_a, 0), wb=_cast_bf16(w_br_b, 0),
               wo=_cast_bf16(w_out, 0), w13=_cast_bf16(w13_src, lf), w2=_cast_bf16(w2_src, lf))

    xs = jnp.concatenate([ctx.reshape(n_ctx, d), x.reshape(seq, d)], axis=0)
    for l in range(depth):
        nxt = {}
        more = l + 1 < depth
        if more:
            w13_src, w2_src, lf = ffn_src(l + 1)

        def ride(kind, *a):
            return (kind, *a) if more else None

        def take(outs, name):
            if more:
                nxt[name] = outs[1]
            return outs[0]

        h = _norm_mod(xs, g_mix, mod, l, 0, n_ctx)
        w_main = wts["w_in"]
        gu = take(_mm_act(h, w_main, 0, 0, o_q, "gelu", BF16, ride=ride("plain", w_br_a, l + 1)), "wa")
        qkv = take(_mm_act(h, w_main, 0, o_q, o_o - o_q, "colscale", BF16, col_scale=q_scale,
                           ride=ride("plain", w_br_b, l + 1)), "wb")
        sig = take(_mm_act(h, w_main, 0, o_o, hw + n_gate, "sigmoid", BF16,
                           ride=ride("w_in", wt_in, l + 1, n_main, n_gate)), "w_in")
        pre = _mm_act(h, w_if, l, 0, LANES, "none", F32)[0]
        hf, hb = _mlstm(qkv, pre, bias_if, l, n_ctx)
        t = _branch_inputs(gu, hf, hb, sig, g_sgu, sgu_w, sgu_bt, g_mh, l, a_width)
        z = _mm_merge(t, wts["wa"], wts["wb"], sig)
        xs = take(_mm_residual(z, wts["wo"], xs, mod, l, 2, n_ctx, ride=ride("plain", w_out, l + 1)), "wo")
        if l % 2 == 0:
            h2 = _norm_mod(xs, g_ffn, mod, l, 3, n_ctx)
            act = take(_ffn_up_dense(h2, wts["w13"], ride=ride("plain", w13_src, lf)), "w13")
        else:
            h2, gates = _norm_mod(xs, g_ffn, mod, l, 3, n_ctx, router=router[l // 2])
            w13 = wts["w13"].reshape(N_EXPERTS, d, -1)
            act = take(_ffn_up_moe(h2, w13, gates, ride=ride("plain", w13_src, lf)), "w13")
        xs = take(_mm_residual(act, wts["w2"], xs, mod, l, 5, n_ctx, ride=ride("plain", w2_src, lf)), "w2")
        wts = nxt
    return _final_norm(xs, final_norm.reshape(1, d), n_ctx).reshape(bsz, seq, d)
```

```python
import functools
import math

import jax
import jax.numpy as jnp
from jax import lax
from jax.experimental import pallas as pl
from jax.experimental.pallas import tpu as pltpu

F32 = jnp.float32
BF16 = jnp.bfloat16

CHUNK = 128
A_GROUPS = 4
M_HEADS = 4
M_DK = 128
M_DV = 256
N_DIRS = 2
IF_WIDTH = N_DIRS * 2 * M_HEADS
GATE_SOFTCAP = 15.0
N_EXPERTS = 8
N_MOD = 6
EPS = 1e-6
MOD_ROWS = 8

LANES = 128
BF16_SUBLANES = 16
V7X_VMEM_BYTES = 64 * 1024 * 1024
VMEM_LIMIT_CAP = V7X_VMEM_BYTES - 3 * 1024 * 1024
MIB = 1024 * 1024


def _params(n_grid_dims, block_bytes, temp_bytes=0, flags=None):
    need = 2 * block_bytes + temp_bytes + 8 * MIB
    return pltpu.CompilerParams(
        dimension_semantics=("arbitrary",) * n_grid_dims,
        vmem_limit_bytes=int(min(max(need, 16 * MIB), VMEM_LIMIT_CAP)), flags=flags)


def _nbytes(shape, dtype):
    return math.prod(shape) * jnp.dtype(dtype).itemsize


def _tile(n, pref, unit=LANES):
    if n <= pref:
        return n
    t = pref - pref % unit
    while n % t:
        t -= unit
    assert t > 0
    return t


def _sigmoid(x):
    return 1.0 / (1.0 + jnp.exp(-x))


def _sigmoid_tanh(x):
    return 0.5 + 0.5 * jnp.tanh(0.5 * x)


def _gelu(x):
    return 0.5 * x * (1.0 + jnp.tanh(0.7978845608028654 * (x + 0.044715 * (x * x * x))))


def _log_sigmoid(x):
    return jnp.minimum(x, 0.0) - jnp.log1p(jnp.exp(-jnp.abs(x)))


def _row_select(mod_ref, tile_rows, tile_index, n_ctx):
    rows = tile_index * tile_rows + lax.broadcasted_iota(jnp.int32, (tile_rows, 1), 0)
    return jnp.where(rows < n_ctx, mod_ref[1:2, :], mod_ref[0:1, :])


def _wspec(w, l, block, index):
    if w.ndim == 2:
        return pl.BlockSpec(block, index)
    return pl.BlockSpec((None,) + block, lambda *g: (l,) + index(*g))


def _cast_kernel(w_ref, o_ref):
    o_ref[...] = w_ref[...].astype(o_ref.dtype)


def _cast_bf16(w, l):
    _, r, c = w.shape
    tr = _tile(r, max(BF16_SUBLANES, (8 * MIB) // (4 * c)), unit=BF16_SUBLANES)
    return pl.pallas_call(
        _cast_kernel, grid=(r // tr,),
        in_specs=[pl.BlockSpec((None, tr, c), lambda i: (l, i, 0))],
        out_specs=pl.BlockSpec((tr, c), lambda i: (i, 0)),
        out_shape=jax.ShapeDtypeStruct((r, c), BF16),
        compiler_params=_params(1, _nbytes((tr, c), F32) + _nbytes((tr, c), BF16)), name="cast_bf16",
    )(w)


def _cast_w_if_kernel(w_ref, o_ref):
    o_ref[...] = jnp.zeros_like(o_ref)
    o_ref[:IF_WIDTH, :] = w_ref[...].astype(BF16)


def _cast_w_if(wt, n_main):
    n_l, _, d = wt.shape
    assert n_main % IF_WIDTH == 0
    return pl.pallas_call(
        _cast_w_if_kernel, grid=(n_l,),
        in_specs=[pl.BlockSpec((None, IF_WIDTH, d), lambda l: (l, n_main // IF_WIDTH, 0))],
        out_specs=pl.BlockSpec((None, LANES, d), lambda l: (l, 0, 0)),
        out_shape=jax.ShapeDtypeStruct((n_l, LANES, d), BF16),
        compiler_params=_params(1, _nbytes((LANES, d), F32)), name="cast_w_if",
    )(wt)


def _w_in_rows(a_ref, b_ref, o_ref, is_gate):
    tr = a_ref.shape[0]
    keep = tr - IF_WIDTH
    o_ref[:keep, :] = jnp.where(is_gate, a_ref[IF_WIDTH:, :], a_ref[:keep, :]).astype(BF16)
    o_ref[keep:, :] = jnp.where(is_gate, b_ref[...], a_ref[keep:, :]).astype(BF16)


def _w_in_specs(wt, l, n_main, n_gate, tr, chunk_of):
    _, n, d = wt.shape
    assert n == n_main + IF_WIDTH + n_gate and n_main % tr == 0 and n_gate % tr == 0 and tr % IF_WIDTH == 0
    per = tr // IF_WIDTH
    ins = [pl.BlockSpec((None, tr, d), lambda *g: (l, chunk_of(*g), 0)),
           pl.BlockSpec((None, IF_WIDTH, d), lambda *g: (l, per * (chunk_of(*g) + 1), 0))]
    out = pl.BlockSpec((tr, d), lambda *g: (chunk_of(*g), 0))
    return ins, out, jax.ShapeDtypeStruct((n_main + n_gate, d), BF16)


def _cast_w_in_kernel(a_ref, b_ref, o_ref, *, main_chunks):
    _w_in_rows(a_ref, b_ref, o_ref, pl.program_id(0) >= main_chunks)


def _cast_w_in(wt, l, n_main, n_gate):
    tr = 512
    ins, out, shape = _w_in_specs(wt, l, n_main, n_gate, tr, lambda i: i)
    d = wt.shape[2]
    return pl.pallas_call(
        functools.partial(_cast_w_in_kernel, main_chunks=n_main // tr), grid=((n_main + n_gate) // tr,),
        in_specs=ins, out_specs=out, out_shape=shape,
        compiler_params=_params(1, _nbytes((tr, d), F32) + _nbytes((tr, d), BF16)), name="cast_w_in",
    )(wt, wt)


def _ride_rows(r, n_steps):
    for cr in range(BF16_SUBLANES, r + 1, BF16_SUBLANES):
        if r % cr == 0 and r // cr <= n_steps:
            return cr
    return None


def _pallas(kern, grid, in_specs, args, out_specs, out_shapes, blk, tmp, name, ride=None):
    n_out = len(out_specs)

    def call(k, ins, outs, shapes, a, extra_blk=0):
        return pl.pallas_call(k, grid=grid, in_specs=ins, out_specs=outs, out_shape=shapes,
                              compiler_params=_params(len(grid), blk + extra_blk, tmp), name=name)(*a)

    if ride is None:
        return call(kern, in_specs, out_specs, out_shapes, args)
    n_steps = math.prod(grid)
    strides = [math.prod(grid[d + 1:]) for d in range(len(grid))]
    n_in = len(in_specs)

    def step(*g):
        return sum(gi * st for gi, st in zip(g, strides))

    if ride[0] == "plain":
        _, src, ls = ride
        _, r, c = src.shape
        cr = _ride_rows(r, n_steps)
        if cr is None:
            return (*call(kern, in_specs, out_specs, out_shapes, args), _cast_bf16(src, ls))
        last = r // cr - 1

        def chunk(*g):
            return jnp.minimum(step(*g), last)

        def riding(*refs):
            src_ref, dst_ref = refs[n_in], refs[n_in + 1 + n_out]
            dst_ref[...] = src_ref[...].astype(BF16)
            kern(*refs[:n_in], *refs[n_in + 1:n_in + 1 + n_out], *refs[n_in + 2 + n_out:])

        return call(riding, in_specs + [pl.BlockSpec((None, cr, c), lambda *g: (ls, chunk(*g), 0))],
                    out_specs + [pl.BlockSpec((cr, c), lambda *g: (chunk(*g), 0))],
                    out_shapes + [jax.ShapeDtypeStruct((r, c), BF16)], args + [src],
                    _nbytes((cr, c), F32) + _nbytes((cr, c), BF16))

    _, wt, ls, n_main, n_gate = ride
    d = wt.shape[2]
    tr = next((t for t in range(IF_WIDTH, n_main + 1, IF_WIDTH)
               if n_main % t == 0 and n_gate % t == 0 and (n_main + n_gate) // t <= n_steps), None)
    if tr is None:
        return (*call(kern, in_specs, out_specs, out_shapes, args), _cast_w_in(wt, ls, n_main, n_gate))
    last = (n_main + n_gate) // tr - 1
    main_chunks = n_main // tr

    def chunk(*g):
        return jnp.minimum(step(*g), last)

    ins, out, shape = _w_in_specs(wt, ls, n_main, n_gate, tr, chunk)

    def riding(*refs):
        a_ref, b_ref, dst_ref = refs[n_in], refs[n_in + 1], refs[n_in + 2 + n_out]
        ids = [pl.program_id(dd) for dd in range(len(grid))]
        _w_in_rows(a_ref, b_ref, dst_ref, chunk(*ids) >= main_chunks)
        kern(*refs[:n_in], *refs[n_in + 2:n_in + 2 + n_out], *refs[n_in + 3 + n_out:])

    return call(riding, in_specs + ins, out_specs + [out], out_shapes + [shape], args + [wt, wt],
                _nbytes((tr, d), F32) + _nbytes((tr, d), BF16))


def _adaln_kernel(cond_ref, w_ref, b_ref, o_ref):
    c = cond_ref[...]
    s = (c * _sigmoid(c)).astype(BF16)
    acc = jnp.dot(s, w_ref[...].astype(BF16), preferred_element_type=F32)
    o_ref[...] = acc + b_ref[...]


def _adaln(cond, ada_w, ada_b):
    n_layers, d, n = ada_w.shape
    r = cond.shape[0]
    tn = _tile(n, 512)
    blk = _nbytes((d, tn), F32) + _nbytes((r, d), F32) + 2 * _nbytes((r, tn), F32)
    return pl.pallas_call(
        _adaln_kernel,
        grid=(n_layers, n // tn),
        in_specs=[pl.BlockSpec((r, d), lambda l, j: (0, 0)),
                  pl.BlockSpec((None, d, tn), lambda l, j: (l, 0, j)),
                  pl.BlockSpec((None, 1, tn), lambda l, j: (l, 0, j))],
        out_specs=pl.BlockSpec((None, r, tn), lambda l, j: (l, 0, j)),
        out_shape=jax.ShapeDtypeStruct((n_layers, r, n), F32),
        compiler_params=_params(2, blk, _nbytes((d, tn), BF16)), name="adaln",
    )(cond, ada_w, ada_b.reshape(n_layers, 1, n))


ROW_GROUP = 16
COL_CHUNK = 1024


def _inv_rms(x_ref, r0, rows):
    d = x_ref.shape[1]
    cc = min(COL_CHUNK, d)
    ss = None
    for c0 in range(0, d, cc):
        xc = x_ref[r0:r0 + rows, c0:c0 + cc]
        part = jnp.sum(xc * xc, axis=-1, keepdims=True)
        ss = part if ss is None else ss + part
    return lax.rsqrt(ss / d + EPS)


def _norm_mod_kernel(x_ref, g_ref, shift_ref, scale_ref, *rest, n_ctx, side):
    tm, d = x_ref.shape
    cc = min(COL_CHUNK, d)
    o_ref = rest[0] if side is None else rest[1]
    row0 = pl.program_id(0) * tm

    def normalise(row_of_group):
        for r0 in range(0, tm, ROW_GROUP):
            mr = row_of_group(r0)
            inv = _inv_rms(x_ref, r0, ROW_GROUP)
            for c0 in range(0, d, cc):
                cs = slice(c0, c0 + cc)
                y = x_ref[r0:r0 + ROW_GROUP, cs] * inv * g_ref[:, cs]
                o_ref[r0:r0 + ROW_GROUP, cs] = (y * (1.0 + scale_ref[pl.ds(mr, 1), cs])
                                                + shift_ref[pl.ds(mr, 1), cs]).astype(BF16)

    @pl.when(row0 >= n_ctx)
    def _():
        normalise(lambda r0: 0)

    @pl.when(row0 < n_ctx)
    def _():
        normalise(lambda r0: jnp.where(row0 + r0 < n_ctx, 1, 0))
    if side is None:
        return
    p_ref, _, side_ref = rest
    if side == "project":
        side_ref[...] = lax.dot_general(o_ref[...], p_ref[...], (((1,), (1,)), ((), ())), preferred_element_type=F32)
        return
    gate_ref = side_ref
    logits = jnp.dot(o_ref[...], p_ref[...], preferred_element_type=F32)
    lane = lax.broadcasted_iota(jnp.int32, logits.shape, 1)
    logits = jnp.where(lane < N_EXPERTS, logits, -jnp.inf)
    m1 = jnp.max(logits, axis=-1, keepdims=True)
    i1 = jnp.min(jnp.where(logits == m1, lane, LANES), axis=-1, keepdims=True)
    rest_l = jnp.where(lane == i1, -jnp.inf, logits)
    m2 = jnp.max(rest_l, axis=-1, keepdims=True)
    i2 = jnp.min(jnp.where(rest_l == m2, lane, LANES), axis=-1, keepdims=True)
    e2 = jnp.exp(m2 - m1)
    gate_ref[...] = jnp.where(lane == i1, 1.0 / (1.0 + e2), 0.0) + jnp.where(lane == i2, e2 / (1.0 + e2), 0.0)


def _norm_mod(x, g, mod, l, k_shift, n_ctx, router=None, project=None):
    m, d = x.shape
    tm = _tile(m, 640, unit=ROW_GROUP)
    assert n_ctx % ROW_GROUP == 0 and m % tm == 0
    in_specs = [pl.BlockSpec((tm, d), lambda i: (i, 0)),
                pl.BlockSpec((None, 1, d), lambda i: (l, 0, 0)),
                pl.BlockSpec((None, MOD_ROWS, d), lambda i: (l, 0, k_shift)),
                pl.BlockSpec((None, MOD_ROWS, d), lambda i: (l, 0, k_shift + 1))]
    blk = _nbytes((tm, d), F32) + _nbytes((tm, d), BF16)
    tmp = 3 * _nbytes((tm, d), F32)
    if router is None and project is None:
        return pl.pallas_call(
            functools.partial(_norm_mod_kernel, n_ctx=n_ctx, side=None), grid=(m // tm,), in_specs=in_specs,
            out_specs=pl.BlockSpec((tm, d), lambda i: (i, 0)),
            out_shape=jax.ShapeDtypeStruct((m, d), BF16),
            compiler_params=_params(1, blk, tmp), name="norm_mod",
        )(x, g, mod, mod)
    if router is not None:
        side, p, p_spec = "router", router, pl.BlockSpec((d, LANES), lambda i: (0, 0))
    else:
        side, p, p_spec = "project", project, pl.BlockSpec((None, LANES, d), lambda i: (l, 0, 0))
    return pl.pallas_call(
        functools.partial(_norm_mod_kernel, n_ctx=n_ctx, side=side), grid=(m // tm,),
        in_specs=in_specs + [p_spec],
        out_specs=[pl.BlockSpec((tm, d), lambda i: (i, 0)), pl.BlockSpec((tm, LANES), lambda i: (i, 0))],
        out_shape=[jax.ShapeDtypeStruct((m, d), BF16), jax.ShapeDtypeStruct((m, LANES), F32)],
        compiler_params=_params(1, blk + _nbytes((d, LANES), BF16), tmp), name="norm_mod_" + side,
    )(x, g, mod, mod, p)


def _mm_act_kernel(x_ref, wt_ref, *rest, act):
    acc = lax.dot_general(x_ref[...], wt_ref[...], (((1,), (1,)), ((), ())), preferred_element_type=F32)
    if act == "gelu":
        acc = _gelu(acc)
    elif act == "sigmoid":
        acc = _sigmoid_tanh(acc)
    elif act == "colscale":
        acc = acc * rest[0][...]
    rest[-1][...] = acc.astype(rest[-1].dtype)


def _mm_act(x, wt, l, row0, n, act, out_dtype, col_scale=None, ride=None, tm_pref=1280, tn_pref=1024):
    m, k = x.shape
    tm, tn = _tile(m, tm_pref), _tile(n, tn_pref)
    assert row0 % tn == 0
    rb = row0 // tn
    in_specs = [pl.BlockSpec((tm, k), lambda i, j: (i, 0)), _wspec(wt, l, (tn, k), lambda i, j: (rb + j, 0))]
    args = [x, wt]
    if act == "colscale":
        in_specs.append(pl.BlockSpec((1, tn), lambda i, j: (0, j)))
        args.append(col_scale)
    blk = _nbytes((tm, k), BF16) + _nbytes((tn, k), BF16) + _nbytes((tm, tn), out_dtype)
    return _pallas(functools.partial(_mm_act_kernel, act=act), (m // tm, n // tn), in_specs, args,
                   [pl.BlockSpec((tm, tn), lambda i, j: (i, j))], [jax.ShapeDtypeStruct((m, n), out_dtype)],
                   blk, 2 * _nbytes((tm, tn), F32), "mm_" + act, ride)


def _mm_residual_kernel(z_ref, w_ref, x_ref, g_ref, o_ref, *, n_ctx):
    acc = jnp.dot(z_ref[...], w_ref[...], preferred_element_type=F32)
    gate = _row_select(g_ref, z_ref.shape[0], pl.program_id(0), n_ctx)
    o_ref[...] = x_ref[...] + gate * acc


def _mm_residual(z, w, x, mod, l_mod, k_gate, n_ctx, ride=None, tm_pref=1280, tn_pref=512):
    m, k = z.shape
    n = w.shape[1]
    tm, tn = _tile(m, tm_pref), _tile(n, tn_pref)
    gb = k_gate * (n // tn)
    blk = _nbytes((tm, k), BF16) + _nbytes((k, tn), BF16) + 2 * _nbytes((tm, tn), F32)
    in_specs = [pl.BlockSpec((tm, k), lambda i, j: (i, 0)), pl.BlockSpec((k, tn), lambda i, j: (0, j)),
                pl.BlockSpec((tm, tn), lambda i, j: (i, j)),
                pl.BlockSpec((None, MOD_ROWS, tn), lambda i, j: (l_mod, 0, gb + j))]
    return _pallas(functools.partial(_mm_residual_kernel, n_ctx=n_ctx), (m // tm, n // tn), in_specs, [z, w, x, mod],
                   [pl.BlockSpec((tm, tn), lambda i, j: (i, j))], [jax.ShapeDtypeStruct((m, n), F32)],
                   blk, 2 * _nbytes((tm, tn), F32), "mm_residual", ride)


def _mm_merge_kernel(ta_ref, tb_ref, wa_ref, wb_ref, ga_ref, gb_ref, o_ref):
    ya = jnp.dot(ta_ref[...], wa_ref[...], preferred_element_type=F32)
    yb = jnp.dot(tb_ref[...], wb_ref[...], preferred_element_type=F32)
    o_ref[...] = (ga_ref[...].astype(F32) * ya + gb_ref[...].astype(F32) * yb).astype(o_ref.dtype)


def _mm_merge(t, wa, wb, sig, tm_pref=1280, tn_pref=1024):
    m = t.shape[0]
    ka, n = wa.shape
    assert wb.shape == wa.shape and t.shape[1] == 2 * ka
    ko = sig.shape[1] - 2 * n
    tm, tn = _tile(m, tm_pref), _tile(n, tn_pref)
    off_a = ko // tn
    off_b = (ko + n) // tn
    assert off_a * tn == ko and off_b * tn == ko + n
    blk = 2 * _nbytes((tm, ka), BF16) + 2 * _nbytes((ka, tn), BF16) + 2 * _nbytes((tm, tn), sig.dtype) \
        + _nbytes((tm, tn), BF16)
    return pl.pallas_call(
        _mm_merge_kernel,
        grid=(m // tm, n // tn),
        in_specs=[pl.BlockSpec((tm, ka), lambda i, j: (i, 0)), pl.BlockSpec((tm, ka), lambda i, j: (i, 1)),
                  pl.BlockSpec((ka, tn), lambda i, j: (0, j)), pl.BlockSpec((ka, tn), lambda i, j: (0, j)),
                  pl.BlockSpec((tm, tn), lambda i, j: (i, j + off_a)),
                  pl.BlockSpec((tm, tn), lambda i, j: (i, j + off_b))],
        out_specs=pl.BlockSpec((tm, tn), lambda i, j: (i, j)),
        out_shape=jax.ShapeDtypeStruct((m, n), BF16),
        compiler_params=_params(2, blk, 3 * _nbytes((tm, tn), F32)), name="mm_merge",
    )(t, t, wa, wb, sig, sig)


def _ffn_up_kernel(h_ref, wg_ref, wu_ref, *rest, routed):
    h = h_ref[...]
    g = jnp.dot(h, wg_ref[...], preferred_element_type=F32)
    u = jnp.dot(h, wu_ref[...], preferred_element_type=F32)
    act = g * _sigmoid_tanh(g) * u
    if routed:
        gates = rest[0][...]
        lane = lax.broadcasted_iota(jnp.int32, gates.shape, 1)
        act = act * jnp.sum(jnp.where(lane == pl.program_id(1), gates, 0.0), axis=-1, keepdims=True)
    rest[-1][...] = act.astype(rest[-1].dtype)


def _ffn_up_dense(h, w13, ride=None, tm_pref=1280, tn_pref=512):
    m, k = h.shape
    f = w13.shape[1] // 2
    tm, tn = _tile(m, tm_pref), _tile(f, tn_pref)
    nb = f // tn
    blk = _nbytes((tm, k), BF16) + 2 * _nbytes((k, tn), BF16) + _nbytes((tm, tn), BF16)
    in_specs = [pl.BlockSpec((tm, k), lambda i, j: (i, 0)), pl.BlockSpec((k, tn), lambda i, j: (0, j)),
                pl.BlockSpec((k, tn), lambda i, j: (0, j + nb))]
    return _pallas(functools.partial(_ffn_up_kernel, routed=False), (m // tm, nb), in_specs, [h, w13, w13],
                   [pl.BlockSpec((tm, tn), lambda i, j: (i, j))], [jax.ShapeDtypeStruct((m, f), BF16)],
                   blk, 3 * _nbytes((tm, tn), F32), "ffn_up_dense", ride)


def _ffn_up_moe(h, w13, gates, ride=None, tm_pref=1280):
    m, k = h.shape
    two_de = w13.shape[2]
    de = two_de // 2
    tm = _tile(m, tm_pref)
    blk = _nbytes((tm, k), BF16) + 2 * _nbytes((k, de), BF16) + _nbytes((tm, de), BF16) + _nbytes((tm, LANES), F32)
    in_specs = [pl.BlockSpec((tm, k), lambda i, j: (i, 0)), pl.BlockSpec((None, k, de), lambda i, j: (j, 0, 0)),
                pl.BlockSpec((None, k, de), lambda i, j: (j, 0, 1)), pl.BlockSpec((tm, LANES), lambda i, j: (i, 0))]
    return _pallas(functools.partial(_ffn_up_kernel, routed=True), (m // tm, N_EXPERTS), in_specs, [h, w13, w13, gates],
                   [pl.BlockSpec((tm, de), lambda i, j: (i, j))], [jax.ShapeDtypeStruct((m, N_EXPERTS * de), BF16)],
                   blk, 3 * _nbytes((tm, de), F32), "ffn_up_moe", ride)


def _split3(x):
    hi = x.astype(BF16)
    r1 = x - hi.astype(F32)
    mid = r1.astype(BF16)
    lo = (r1 - mid.astype(F32)).astype(BF16)
    return hi, mid, lo


def _mlstm_step(qkv_f_ref, qkv_b_ref, pre_f_ref, pre_b_ref, bias_ref, hf_ref, hb_ref, c_ref, n_ref, m_ref, offs):
    rows = lax.broadcasted_iota(jnp.int32, (CHUNK, CHUNK), 0)
    cols = lax.broadcasted_iota(jnp.int32, (CHUNK, CHUNK), 1)
    for d, (qkv_ref, pre_ref, out_ref) in enumerate(((qkv_f_ref, pre_f_ref, hf_ref), (qkv_b_ref, pre_b_ref, hb_ref))):
        rs = slice(offs[d], offs[d] + CHUNK)
        mask = (rows >= cols) if d == 0 else (rows <= cols)
        tri = jnp.where(mask, 1.0, 0.0).astype(BF16)
        a = pre_ref[rs, :] + bias_ref[...]
        a = GATE_SOFTCAP * jnp.tanh(a / GATE_SOFTCAP)
        lf = _log_sigmoid(a)
        hi, mid, lo = _split3(lf)
        b_all = (jnp.dot(tri, hi, preferred_element_type=F32) + jnp.dot(tri, mid, preferred_element_type=F32)
                 + jnp.dot(tri, lo, preferred_element_type=F32))
        a_t = a.T
        b_t = b_all.T
        last = CHUNK - 1 if d == 0 else 0
        for hd in range(M_HEADS):
            ci = d * 2 * M_HEADS + hd
            cf = ci + M_HEADS
            r = d * M_HEADS + hd
            ig_col, ig_row = a[:, ci:ci + 1], a_t[ci:ci + 1, :]
            b_col, b_row = b_all[:, cf:cf + 1], b_t[cf:cf + 1, :]
            b_last = b_col[last:last + 1, :]
            m_prev = m_ref[r:r + 1, 0:1]
            c_prev = c_ref[r]
            n_prev = n_ref[r:r + 1, :]
            q = qkv_ref[rs, hd * M_DK:(hd + 1) * M_DK]
            k = qkv_ref[rs, M_HEADS * M_DK + hd * M_DK:M_HEADS * M_DK + (hd + 1) * M_DK]
            v = qkv_ref[rs, 2 * M_HEADS * M_DK + hd * M_DV:2 * M_HEADS * M_DK + (hd + 1) * M_DV]

            log_d = jnp.where(mask, b_col - b_row + ig_row, -jnp.inf)
            m_inter = b_col + m_prev
            m_row = jnp.maximum(m_inter, jnp.max(log_d, axis=-1, keepdims=True))
            qk = lax.dot_general(q, k, (((1,), (1,)), ((), ())), preferred_element_type=F32)
            s = qk * jnp.exp(log_d - m_row)
            w_inter = jnp.exp(m_inter - m_row)
            qc = jnp.dot(q, c_prev.astype(BF16), preferred_element_type=F32)
            num = jnp.dot(s.astype(BF16), v, preferred_element_type=F32) + w_inter * qc
            qn = jnp.sum(q.astype(F32) * n_prev, axis=-1, keepdims=True)
            den = jnp.sum(s, axis=-1, keepdims=True) + w_inter * qn
            out_ref[rs, hd * M_DV:(hd + 1) * M_DV] = num / jnp.maximum(jnp.abs(den), jnp.exp(-m_row))

            log_w = b_last - b_col + ig_col
            m_new = jnp.maximum(b_last + m_prev, jnp.max(log_w, axis=0, keepdims=True))
            w = jnp.exp(log_w - m_new)
            decay = jnp.exp(b_last + m_prev - m_new)
            wk = w * k.astype(F32)
            c_ref[r] = decay * c_prev + lax.dot_general(wk.astype(BF16), v, (((0,), (0,)), ((), ())),
                                                        preferred_element_type=F32)
            n_ref[r:r + 1, :] = decay * n_prev + jnp.sum(wk, axis=0, keepdims=True)
            m_ref[r:r + 1, :] = jnp.broadcast_to(m_new, (1, LANES))


SCAN_STATE_SHAPES = ((N_DIRS * M_HEADS, M_DK, M_DV), (N_DIRS * M_HEADS, M_DK), (N_DIRS * M_HEADS, LANES))


def _mlstm_kernel(qkv_f_ref, qkv_b_ref, pre_f_ref, pre_b_ref, bias_ref, hf_ref, hb_ref, c_ref, n_ref, m_ref):
    @pl.when(pl.program_id(0) == 0)
    def _():
        c_ref[...] = jnp.zeros_like(c_ref)
        n_ref[...] = jnp.zeros_like(n_ref)
        m_ref[...] = jnp.zeros_like(m_ref)

    per_step = qkv_f_ref.shape[0] // CHUNK
    for p in range(per_step):
        _mlstm_step(qkv_f_ref, qkv_b_ref, pre_f_ref, pre_b_ref, bias_ref, hf_ref, hb_ref, c_ref, n_ref, m_ref,
                    offs=(p * CHUNK, (per_step - 1 - p) * CHUNK))


def _mlstm(qkv, pre, bias, l, n_ctx):
    rows, width = qkv.shape
    per_step = 1
    tr = per_step * CHUNK
    nc = rows // tr
    ncc = n_ctx // tr
    hw = M_HEADS * M_DV
    fwd = lambda t: (t, 0)
    bwd = lambda t: (jnp.where(t < ncc, ncc - 1 - t, ncc + nc - 1 - t), 0)
    blk = 2 * _nbytes((tr, width), BF16) + 2 * _nbytes((tr, LANES), F32) + 2 * _nbytes((tr, hw), F32)
    return pl.pallas_call(
        _mlstm_kernel,
        grid=(nc,),
        in_specs=[pl.BlockSpec((tr, width), fwd), pl.BlockSpec((tr, width), bwd),
                  pl.BlockSpec((tr, LANES), fwd), pl.BlockSpec((tr, LANES), bwd),
                  pl.BlockSpec((None, 1, LANES), lambda t: (l, 0, 0))],
        out_specs=[pl.BlockSpec((tr, hw), fwd), pl.BlockSpec((tr, hw), bwd)],
        out_shape=[jax.ShapeDtypeStruct((rows, hw), F32), jax.ShapeDtypeStruct((rows, hw), F32)],
        scratch_shapes=[pltpu.VMEM(s, F32) for s in SCAN_STATE_SHAPES],
        compiler_params=_params(1, blk, 10 * MIB), name="mlstm_scan",
    )(qkv, qkv, pre, pre, bias)


def _branch_kernel(gu_ref, hf_ref, hb_ref, osig_ref, sgu_g_ref, sgu_w_ref, sgu_bt_ref, mh_g_ref, o_ref, *, a_width):
    gd = a_width // A_GROUPS
    for r0 in range(0, gu_ref.shape[0], CHUNK):
        rs = slice(r0, r0 + CHUNK)
        v = gu_ref[rs, a_width:2 * a_width].astype(F32)
        vc = v - jnp.mean(v, axis=-1, keepdims=True)
        vn = (vc * lax.rsqrt(jnp.mean(vc * vc, axis=-1, keepdims=True) + EPS) * sgu_g_ref[...]).astype(BF16)
        for g in range(A_GROUPS):
            sl = slice(g * gd, (g + 1) * gd)
            s = jnp.dot(sgu_w_ref[g].astype(BF16), vn[:, sl], preferred_element_type=F32) + sgu_bt_ref[:, g:g + 1]
            o_ref[rs, sl] = (gu_ref[rs, sl].astype(F32) * s).astype(o_ref.dtype)
        for hd in range(M_HEADS):
            sl = slice(hd * M_DV, (hd + 1) * M_DV)
            h = hf_ref[rs, sl] + hb_ref[rs, sl]
            hn = h * lax.rsqrt(jnp.mean(h * h, axis=-1, keepdims=True) + EPS)
            o_ref[rs, a_width + hd * M_DV:a_width + (hd + 1) * M_DV] = \
                (hn * mh_g_ref[:, sl] * osig_ref[rs, sl].astype(F32)).astype(o_ref.dtype)


def _branch_inputs(gu, hf, hb, sig, sgu_g, sgu_w, sgu_bt, mh_g, l, a_width):
    n = gu.shape[0]
    hw = M_HEADS * M_DV
    tr = _tile(n, 640, unit=CHUNK)
    blk = _nbytes((tr, 2 * a_width), gu.dtype) + 2 * _nbytes((tr, hw), F32) + _nbytes((tr, hw), sig.dtype) \
        + _nbytes((tr, a_width + hw), BF16)
    return pl.pallas_call(
        functools.partial(_branch_kernel, a_width=a_width),
        grid=(n // tr,),
        in_specs=[pl.BlockSpec((tr, 2 * a_width), lambda t: (t, 0)),
                  pl.BlockSpec((tr, hw), lambda t: (t, 0)), pl.BlockSpec((tr, hw), lambda t: (t, 0)),
                  pl.BlockSpec((tr, hw), lambda t: (t, 0)),
                  pl.BlockSpec((None, 1, a_width), lambda t: (l, 0, 0)),
                  pl.BlockSpec((None, A_GROUPS, CHUNK, CHUNK), lambda t: (l, 0, 0, 0)),
                  pl.BlockSpec((None, CHUNK, LANES), lambda t: (l, 0, 0)),
                  pl.BlockSpec((None, 1, hw), lambda t: (l, 0, 0))],
        out_specs=pl.BlockSpec((tr, a_width + hw), lambda t: (t, 0)),
        out_shape=jax.ShapeDtypeStruct((n, a_width + hw), BF16),
        compiler_params=_params(1, blk, 6 * MIB), name="branch_inputs",
    )(gu, hf, hb, sig, sgu_g, sgu_w, sgu_bt, mh_g)


def _final_norm_kernel(x_ref, g_ref, o_ref):
    tm, d = x_ref.shape
    cc = min(COL_CHUNK, d)
    for r0 in range(0, tm, ROW_GROUP):
        inv = _inv_rms(x_ref, r0, ROW_GROUP)
        for c0 in range(0, d, cc):
            cs = slice(c0, c0 + cc)
            o_ref[r0:r0 + ROW_GROUP, cs] = x_ref[r0:r0 + ROW_GROUP, cs] * inv * g_ref[:, cs]


def _final_norm(x, g, n_ctx):
    m, d = x.shape
    tm = _tile(n_ctx, 256, unit=ROW_GROUP)
    skip = n_ctx // tm
    assert skip * tm == n_ctx and (m - n_ctx) % tm == 0
    return pl.pallas_call(
        _final_norm_kernel, grid=((m - n_ctx) // tm,),
        in_specs=[pl.BlockSpec((tm, d), lambda i: (i + skip, 0)), pl.BlockSpec((1, d), lambda i: (0, 0))],
        out_specs=pl.BlockSpec((tm, d), lambda i: (i, 0)),
        out_shape=jax.ShapeDtypeStruct((m - n_ctx, d), F32),
        compiler_params=_params(1, 2 * _nbytes((tm, d), F32), 2 * _nbytes((tm, d), F32)), name="final_norm",
    )(x, g)


def kernel(x, c, ctx, c_ctx, ada_w, ada_b, norm_mix, w_in, sgu_norm, sgu_w, sgu_b, gate_bias, mh_norm, w_br_a, w_br_b, w_out, norm_ffn, dense_w13, dense_w2, moe_router, moe_w13, moe_w2, final_norm):
    bsz, seq, d = x.shape
    n_ctx = ctx.shape[1]
    assert bsz == 1 and seq % CHUNK == 0 and n_ctx % CHUNK == 0
    depth = w_in.shape[0]
    a_width = sgu_norm.shape[1]
    qk_w = M_HEADS * M_DK
    hw = M_HEADS * M_DV
    n_main = 2 * a_width + 2 * qk_w + 2 * hw
    n_gate = 2 * d
    assert mh_norm.shape[1] == hw and a_width == hw and w_in.shape[2] == n_main + IF_WIDTH + n_gate
    o_q, o_o = 2 * a_width, 2 * a_width + 2 * qk_w + hw

    wt_in = jnp.swapaxes(w_in, 1, 2)
    moe_w13_rows = moe_w13.reshape(moe_w13.shape[0], -1, moe_w13.shape[3])
    moe_w2_rows = moe_w2.reshape(moe_w2.shape[0], -1, d)

    def ffn_src(l):
        return (dense_w13, dense_w2, l // 2) if l % 2 == 0 else (moe_w13_rows, moe_w2_rows, l // 2)

    w_if = _cast_w_if(wt_in, n_main)
    router = jnp.pad(moe_router, ((0, 0), (0, 0), (0, LANES - N_EXPERTS))).astype(BF16)
    cond = jnp.zeros((MOD_ROWS, d), F32).at[0].set(c[0]).at[1].set(c_ctx)
    mod = _adaln(cond, ada_w, ada_b)

    q_scale = jnp.concatenate([jnp.full((1, qk_w), M_DK ** -0.5, F32), jnp.ones((1, qk_w + hw), F32)], axis=1)
    bias_if = jnp.pad(gate_bias, ((0, 0), (0, LANES - IF_WIDTH))).reshape(depth, 1, LANES)
    sgu_bt = jnp.pad(jnp.swapaxes(sgu_b, 1, 2), ((0, 0), (0, 0), (0, LANES - A_GROUPS)))
    g_mix, g_ffn = norm_mix.reshape(depth, 1, d), norm_ffn.reshape(depth, 1, d)
    g_sgu, g_mh = sgu_norm.reshape(depth, 1, a_width), mh_norm.reshape(depth, 1, hw)

    w13_src, w2_src, lf = ffn_src(0)
    wts = dict(w_in=_cast_w_in(wt_in, 0, n_main, n_gate), wa=_cast_bf16(w_br_a, 0), wb=_cast_bf16(w_br_b, 0),
               wo=_cast_bf16(w_out, 0), w13=_cast_bf16(w13_src, lf), w2=_cast_bf16(w2_src, lf))

    xs = jnp.concatenate([ctx.reshape(n_ctx, d), x.reshape(seq, d)], axis=0)
    for l in range(depth):
        nxt = {}
        more = l + 1 < depth
        if more:
            w13_src, w2_src, lf = ffn_src(l + 1)

        def ride(kind, *a):
            return (kind, *a) if more else None

        def take(outs, name):
            if more:
                nxt[name] = outs[1]
            return outs[0]

        h, pre = _norm_mod(xs, g_mix, mod, l, 0, n_ctx, project=w_if)
        w_main = wts["w_in"]
        gu = take(_mm_act(h, w_main, 0, 0, o_q, "gelu", BF16, ride=ride("plain", w_br_a, l + 1)), "wa")
        qkv = take(_mm_act(h, w_main, 0, o_q, o_o - o_q, "colscale", BF16, col_scale=q_scale,
                           ride=ride("plain", w_br_b, l + 1)), "wb")
        sig = take(_mm_act(h, w_main, 0, o_o, hw + n_gate, "sigmoid", BF16,
                           ride=ride("w_in", wt_in, l + 1, n_main, n_gate)), "w_in")
        hf, hb = _mlstm(qkv, pre, bias_if, l, n_ctx)
        t = _branch_inputs(gu, hf, hb, sig, g_sgu, sgu_w, sgu_bt, g_mh, l, a_width)
        z = _mm_merge(t, wts["wa"], wts["wb"], sig)
        xs = take(_mm_residual(z, wts["wo"], xs, mod, l, 2, n_ctx, ride=ride("plain", w_out, l + 1)), "wo")
        if l % 2 == 0:
            h2 = _norm_mod(xs, g_ffn, mod, l, 3, n_ctx)
            act = take(_ffn_up_dense(h2, wts["w13"], ride=ride("plain", w13_src, lf)), "w13")
        else:
            h2, gates = _norm_mod(xs, g_ffn, mod, l, 3, n_ctx, router=router[l // 2])
            w13 = wts["w13"].reshape(N_EXPERTS, d, -1)
            act = take(_ffn_up_moe(h2, w13, gates, ride=ride("plain", w13_src, lf)), "w13")
        xs = take(_mm_residual(act, wts["w2"], xs, mod, l, 5, n_ctx, ride=ride("plain", w2_src, lf)), "w2")
        wts = nxt
    return _final_norm(xs, final_norm.reshape(1, d), n_ctx).reshape(bsz, seq, d)
```

```python
import functools
import math

import jax
import jax.numpy as jnp
from jax import lax
from jax.experimental import pallas as pl
from jax.experimental.pallas import tpu as pltpu

F32 = jnp.float32
BF16 = jnp.bfloat16

CHUNK = 128
A_GROUPS = 4
M_HEADS = 4
M_DK = 128
M_DV = 256
N_DIRS = 2
IF_WIDTH = N_DIRS * 2 * M_HEADS
GATE_SOFTCAP = 15.0
N_EXPERTS = 8
EPS = 1e-6
MOD_ROWS = 8

LANES = 128
BF16_SUBLANES = 16
V7X_VMEM_BYTES = 64 * 1024 * 1024
MIB = 1024 * 1024
VMEM_LIMIT_CAP = V7X_VMEM_BYTES - 3 * MIB
VMEM_SLACK_BYTES = 8 * MIB

MM_ROWS = 1280
MM_COLS = 1024
MM_COLS_NARROW = 512
STREAM_ROWS = 640
CAST_ROWS = 512
CAST_BLOCK_BYTES = 8 * MIB
SCAN_TEMP_BYTES = 10 * MIB
BRANCH_TEMP_BYTES = 6 * MIB


def _params(n_grid_dims, block_bytes, temp_bytes=0):
    need = 2 * block_bytes + temp_bytes + VMEM_SLACK_BYTES
    return pltpu.CompilerParams(
        dimension_semantics=("arbitrary",) * n_grid_dims,
        vmem_limit_bytes=int(min(max(need, 16 * MIB), VMEM_LIMIT_CAP)))


def _nbytes(shape, dtype):
    return math.prod(shape) * jnp.dtype(dtype).itemsize


def _tile(n, pref, unit=LANES):
    if n <= pref:
        return n
    t = pref - pref % unit
    while n % t:
        t -= unit
    assert t > 0
    return t


def _sigmoid(x):
    return 1.0 / (1.0 + jnp.exp(-x))


def _sigmoid_tanh(x):
    return 0.5 + 0.5 * jnp.tanh(0.5 * x)


def _gelu(x):
    return 0.5 * x * (1.0 + jnp.tanh(0.7978845608028654 * (x + 0.044715 * (x * x * x))))


def _log_sigmoid(x):
    return jnp.minimum(x, 0.0) - jnp.log1p(jnp.exp(-jnp.abs(x)))


def _row_select(mod_ref, tile_rows, tile_index, n_ctx):
    rows = tile_index * tile_rows + lax.broadcasted_iota(jnp.int32, (tile_rows, 1), 0)
    return jnp.where(rows < n_ctx, mod_ref[1:2, :], mod_ref[0:1, :])


def _wspec(w, l, block, index):
    if w.ndim == 2:
        return pl.BlockSpec(block, index)
    return pl.BlockSpec((None,) + block, lambda *g: (l,) + index(*g))


def _cast_kernel(w_ref, o_ref):
    o_ref[...] = w_ref[...].astype(o_ref.dtype)


def _cast_bf16(w, l):
    _, r, c = w.shape
    tr = _tile(r, max(BF16_SUBLANES, CAST_BLOCK_BYTES // (4 * c)), unit=BF16_SUBLANES)
    return pl.pallas_call(
        _cast_kernel, grid=(r // tr,),
        in_specs=[pl.BlockSpec((None, tr, c), lambda i: (l, i, 0))],
        out_specs=pl.BlockSpec((tr, c), lambda i: (i, 0)),
        out_shape=jax.ShapeDtypeStruct((r, c), BF16),
        compiler_params=_params(1, _nbytes((tr, c), F32) + _nbytes((tr, c), BF16)), name="cast_bf16",
    )(w)


def _cast_w_if_kernel(w_ref, o_ref):
    o_ref[...] = jnp.zeros_like(o_ref)
    o_ref[:IF_WIDTH, :] = w_ref[...].astype(BF16)


def _cast_w_if(wt, n_main):
    n_l, _, d = wt.shape
    assert n_main % IF_WIDTH == 0
    return pl.pallas_call(
        _cast_w_if_kernel, grid=(n_l,),
        in_specs=[pl.BlockSpec((None, IF_WIDTH, d), lambda l: (l, n_main // IF_WIDTH, 0))],
        out_specs=pl.BlockSpec((None, LANES, d), lambda l: (l, 0, 0)),
        out_shape=jax.ShapeDtypeStruct((n_l, LANES, d), BF16),
        compiler_params=_params(1, _nbytes((LANES, d), F32)), name="cast_w_if",
    )(wt)


def _w_in_rows(a_ref, b_ref, o_ref, is_gate):
    tr = a_ref.shape[0]
    keep = tr - IF_WIDTH
    first = pl.multiple_of(jnp.where(is_gate, IF_WIDTH, 0), IF_WIDTH)
    o_ref[:keep, :] = a_ref[pl.ds(first, keep), :].astype(BF16)
    o_ref[keep:, :] = jnp.where(is_gate, b_ref[...], a_ref[keep:, :]).astype(BF16)


def _w_in_specs(wt, l, n_main, n_gate, tr, chunk_of):
    _, n, d = wt.shape
    assert n == n_main + IF_WIDTH + n_gate and n_main % tr == 0 and n_gate % tr == 0 and tr % IF_WIDTH == 0
    per = tr // IF_WIDTH
    ins = [pl.BlockSpec((None, tr, d), lambda *g: (l, chunk_of(*g), 0)),
           pl.BlockSpec((None, IF_WIDTH, d), lambda *g: (l, per * (chunk_of(*g) + 1), 0))]
    out = pl.BlockSpec((tr, d), lambda *g: (chunk_of(*g), 0))
    return ins, out, jax.ShapeDtypeStruct((n_main + n_gate, d), BF16)


def _cast_w_in_kernel(a_ref, b_ref, o_ref, *, main_chunks):
    _w_in_rows(a_ref, b_ref, o_ref, pl.program_id(0) >= main_chunks)


def _cast_w_in(wt, l, n_main, n_gate):
    tr = math.gcd(n_main, n_gate, CAST_ROWS)
    ins, out, shape = _w_in_specs(wt, l, n_main, n_gate, tr, lambda i: i)
    d = wt.shape[2]
    return pl.pallas_call(
        functools.partial(_cast_w_in_kernel, main_chunks=n_main // tr), grid=((n_main + n_gate) // tr,),
        in_specs=ins, out_specs=out, out_shape=shape,
        compiler_params=_params(1, _nbytes((tr, d), F32) + _nbytes((tr, d), BF16)), name="cast_w_in",
    )(wt, wt)


def _ride_rows(r, n_steps):
    for cr in range(BF16_SUBLANES, r + 1, BF16_SUBLANES):
        if r % cr == 0 and r // cr <= n_steps:
            return cr
    return None


def _pallas(kern, grid, in_specs, args, out_specs, out_shapes, blk, tmp, name, ride=None):
    n_out = len(out_specs)

    def call(k, ins, outs, shapes, a, extra_blk=0):
        return pl.pallas_call(k, grid=grid, in_specs=ins, out_specs=outs, out_shape=shapes,
                              compiler_params=_params(len(grid), blk + extra_blk, tmp), name=name)(*a)

    if ride is None:
        return call(kern, in_specs, out_specs, out_shapes, args)
    n_steps = math.prod(grid)
    strides = [math.prod(grid[d + 1:]) for d in range(len(grid))]
    n_in = len(in_specs)

    def step(*g):
        return sum(gi * st for gi, st in zip(g, strides))

    if ride[0] == "plain":
        _, src, ls = ride
        _, r, c = src.shape
        cr = _ride_rows(r, n_steps)
        if cr is None:
            return (*call(kern, in_specs, out_specs, out_shapes, args), _cast_bf16(src, ls))
        last = r // cr - 1

        def chunk(*g):
            return jnp.minimum(step(*g), last)

        def riding(*refs):
            src_ref, dst_ref = refs[n_in], refs[n_in + 1 + n_out]
            dst_ref[...] = src_ref[...].astype(BF16)
            kern(*refs[:n_in], *refs[n_in + 1:n_in + 1 + n_out], *refs[n_in + 2 + n_out:])

        return call(riding, in_specs + [pl.BlockSpec((None, cr, c), lambda *g: (ls, chunk(*g), 0))],
                    out_specs + [pl.BlockSpec((cr, c), lambda *g: (chunk(*g), 0))],
                    out_shapes + [jax.ShapeDtypeStruct((r, c), BF16)], args + [src],
                    _nbytes((cr, c), F32) + _nbytes((cr, c), BF16))

    _, wt, ls, n_main, n_gate = ride
    d = wt.shape[2]
    tr = next((t for t in range(IF_WIDTH, n_main + 1, IF_WIDTH)
               if n_main % t == 0 and n_gate % t == 0 and (n_main + n_gate) // t <= n_steps), None)
    if tr is None:
        return (*call(kern, in_specs, out_specs, out_shapes, args), _cast_w_in(wt, ls, n_main, n_gate))
    last = (n_main + n_gate) // tr - 1
    main_chunks = n_main // tr

    def chunk(*g):
        return jnp.minimum(step(*g), last)

    ins, out, shape = _w_in_specs(wt, ls, n_main, n_gate, tr, chunk)

    def riding(*refs):
        a_ref, b_ref, dst_ref = refs[n_in], refs[n_in + 1], refs[n_in + 2 + n_out]
        ids = [pl.program_id(dd) for dd in range(len(grid))]
        _w_in_rows(a_ref, b_ref, dst_ref, chunk(*ids) >= main_chunks)
        kern(*refs[:n_in], *refs[n_in + 2:n_in + 2 + n_out], *refs[n_in + 3 + n_out:])

    return call(riding, in_specs + ins, out_specs + [out], out_shapes + [shape], args + [wt, wt],
                _nbytes((tr, d), F32) + _nbytes((tr, d), BF16))


def _adaln_kernel(cond_ref, w_ref, b_ref, o_ref):
    c = cond_ref[...]
    s = (c * _sigmoid(c)).astype(BF16)
    acc = jnp.dot(s, w_ref[...].astype(BF16), preferred_element_type=F32)
    o_ref[...] = acc + b_ref[...]


def _adaln(cond, ada_w, ada_b):
    n_layers, d, n = ada_w.shape
    r = cond.shape[0]
    tn = _tile(n, MM_COLS_NARROW)
    blk = _nbytes((d, tn), F32) + _nbytes((r, d), F32) + 2 * _nbytes((r, tn), F32)
    return pl.pallas_call(
        _adaln_kernel,
        grid=(n_layers, n // tn),
        in_specs=[pl.BlockSpec((r, d), lambda l, j: (0, 0)),
                  pl.BlockSpec((None, d, tn), lambda l, j: (l, 0, j)),
                  pl.BlockSpec((None, 1, tn), lambda l, j: (l, 0, j))],
        out_specs=pl.BlockSpec((None, r, tn), lambda l, j: (l, 0, j)),
        out_shape=jax.ShapeDtypeStruct((n_layers, r, n), F32),
        compiler_params=_params(2, blk, _nbytes((d, tn), BF16)), name="adaln",
    )(cond, ada_w, ada_b.reshape(n_layers, 1, n))


ROW_GROUP = 16
COL_CHUNK = 1024


def _inv_rms(x_ref, r0, rows):
    d = x_ref.shape[1]
    cc = min(COL_CHUNK, d)
    ss = None
    for c0 in range(0, d, cc):
        xc = x_ref[r0:r0 + rows, c0:c0 + cc]
        part = jnp.sum(xc * xc, axis=-1, keepdims=True)
        ss = part if ss is None else ss + part
    return lax.rsqrt(ss / d + EPS)


def _norm_mod_kernel(x_ref, g_ref, shift_ref, scale_ref, *rest, n_ctx, side):
    tm, d = x_ref.shape
    cc = min(COL_CHUNK, d)
    o_ref = rest[0] if side is None else rest[1]
    row0 = pl.program_id(0) * tm

    def normalise(row_of_group):
        for r0 in range(0, tm, ROW_GROUP):
            mr = row_of_group(r0)
            inv = _inv_rms(x_ref, r0, ROW_GROUP)
            for c0 in range(0, d, cc):
                cs = slice(c0, c0 + cc)
                y = x_ref[r0:r0 + ROW_GROUP, cs] * inv * g_ref[:, cs]
                o_ref[r0:r0 + ROW_GROUP, cs] = (y * (1.0 + scale_ref[pl.ds(mr, 1), cs])
                                                + shift_ref[pl.ds(mr, 1), cs]).astype(BF16)

    @pl.when(row0 >= n_ctx)
    def _():
        normalise(lambda r0: 0)

    @pl.when(row0 < n_ctx)
    def _():
        normalise(lambda r0: jnp.where(row0 + r0 < n_ctx, 1, 0))
    if side is None:
        return
    p_ref, _, side_ref = rest
    if side == "project":
        side_ref[...] = lax.dot_general(o_ref[...], p_ref[...], (((1,), (1,)), ((), ())), preferred_element_type=F32)
        return
    gate_ref = side_ref
    logits = jnp.dot(o_ref[...], p_ref[...], preferred_element_type=F32)
    lane = lax.broadcasted_iota(jnp.int32, logits.shape, 1)
    logits = jnp.where(lane < N_EXPERTS, logits, -jnp.inf)
    m1 = jnp.max(logits, axis=-1, keepdims=True)
    i1 = jnp.min(jnp.where(logits == m1, lane, LANES), axis=-1, keepdims=True)
    rest_l = jnp.where(lane == i1, -jnp.inf, logits)
    m2 = jnp.max(rest_l, axis=-1, keepdims=True)
    i2 = jnp.min(jnp.where(rest_l == m2, lane, LANES), axis=-1, keepdims=True)
    e2 = jnp.exp(m2 - m1)
    gate_ref[...] = jnp.where(lane == i1, 1.0 / (1.0 + e2), 0.0) + jnp.where(lane == i2, e2 / (1.0 + e2), 0.0)


def _norm_mod(x, g, mod, l, k_shift, n_ctx, router=None, project=None):
    m, d = x.shape
    tm = _tile(m, STREAM_ROWS, unit=ROW_GROUP)
    assert n_ctx % ROW_GROUP == 0 and m % tm == 0
    in_specs = [pl.BlockSpec((tm, d), lambda i: (i, 0)),
                pl.BlockSpec((None, 1, d), lambda i: (l, 0, 0)),
                pl.BlockSpec((None, MOD_ROWS, d), lambda i: (l, 0, k_shift)),
                pl.BlockSpec((None, MOD_ROWS, d), lambda i: (l, 0, k_shift + 1))]
    blk = _nbytes((tm, d), F32) + _nbytes((tm, d), BF16)
    tmp = 3 * _nbytes((tm, d), F32)
    if router is None and project is None:
        return pl.pallas_call(
            functools.partial(_norm_mod_kernel, n_ctx=n_ctx, side=None), grid=(m // tm,), in_specs=in_specs,
            out_specs=pl.BlockSpec((tm, d), lambda i: (i, 0)),
            out_shape=jax.ShapeDtypeStruct((m, d), BF16),
            compiler_params=_params(1, blk, tmp), name="norm_mod",
        )(x, g, mod, mod)
    if router is not None:
        side, p, p_spec = "router", router, pl.BlockSpec((d, LANES), lambda i: (0, 0))
    else:
        side, p, p_spec = "project", project, pl.BlockSpec((None, LANES, d), lambda i: (l, 0, 0))
    return pl.pallas_call(
        functools.partial(_norm_mod_kernel, n_ctx=n_ctx, side=side), grid=(m // tm,),
        in_specs=in_specs + [p_spec],
        out_specs=[pl.BlockSpec((tm, d), lambda i: (i, 0)), pl.BlockSpec((tm, LANES), lambda i: (i, 0))],
        out_shape=[jax.ShapeDtypeStruct((m, d), BF16), jax.ShapeDtypeStruct((m, LANES), F32)],
        compiler_params=_params(1, blk + _nbytes((d, LANES), BF16), tmp), name="norm_mod_" + side,
    )(x, g, mod, mod, p)


def _mm_act_kernel(x_ref, wt_ref, *rest, act):
    acc = lax.dot_general(x_ref[...], wt_ref[...], (((1,), (1,)), ((), ())), preferred_element_type=F32)
    if act == "gelu":
        acc = _gelu(acc)
    elif act == "sigmoid":
        acc = _sigmoid_tanh(acc)
    elif act == "colscale":
        acc = acc * rest[0][...]
    rest[-1][...] = acc.astype(rest[-1].dtype)


def _mm_act(x, wt, l, row0, n, act, out_dtype, col_scale=None, ride=None, tm_pref=MM_ROWS, tn_pref=MM_COLS):
    m, k = x.shape
    tm, tn = _tile(m, tm_pref), _tile(n, tn_pref)
    assert row0 % tn == 0
    rb = row0 // tn
    in_specs = [pl.BlockSpec((tm, k), lambda i, j: (i, 0)), _wspec(wt, l, (tn, k), lambda i, j: (rb + j, 0))]
    args = [x, wt]
    if act == "colscale":
        in_specs.append(pl.BlockSpec((1, tn), lambda i, j: (0, j)))
        args.append(col_scale)
    blk = _nbytes((tm, k), BF16) + _nbytes((tn, k), BF16) + _nbytes((tm, tn), out_dtype)
    return _pallas(functools.partial(_mm_act_kernel, act=act), (m // tm, n // tn), in_specs, args,
                   [pl.BlockSpec((tm, tn), lambda i, j: (i, j))], [jax.ShapeDtypeStruct((m, n), out_dtype)],
                   blk, 2 * _nbytes((tm, tn), F32), "mm_" + act, ride)


def _mm_residual_kernel(z_ref, w_ref, x_ref, g_ref, o_ref, *, n_ctx):
    acc = jnp.dot(z_ref[...], w_ref[...], preferred_element_type=F32)
    gate = _row_select(g_ref, z_ref.shape[0], pl.program_id(0), n_ctx)
    o_ref[...] = x_ref[...] + gate * acc


def _mm_residual(z, w, x, mod, l_mod, k_gate, n_ctx, ride=None, tm_pref=MM_ROWS, tn_pref=MM_COLS_NARROW):
    m, k = z.shape
    n = w.shape[1]
    tm, tn = _tile(m, tm_pref), _tile(n, tn_pref)
    gb = k_gate * (n // tn)
    blk = _nbytes((tm, k), BF16) + _nbytes((k, tn), BF16) + 2 * _nbytes((tm, tn), F32)
    in_specs = [pl.BlockSpec((tm, k), lambda i, j: (i, 0)), pl.BlockSpec((k, tn), lambda i, j: (0, j)),
                pl.BlockSpec((tm, tn), lambda i, j: (i, j)),
                pl.BlockSpec((None, MOD_ROWS, tn), lambda i, j: (l_mod, 0, gb + j))]
    return _pallas(functools.partial(_mm_residual_kernel, n_ctx=n_ctx), (m // tm, n // tn), in_specs, [z, w, x, mod],
                   [pl.BlockSpec((tm, tn), lambda i, j: (i, j))], [jax.ShapeDtypeStruct((m, n), F32)],
                   blk, 2 * _nbytes((tm, tn), F32), "mm_residual", ride)


def _mm_merge_kernel(ta_ref, tb_ref, wa_ref, wb_ref, ga_ref, gb_ref, o_ref):
    ya = jnp.dot(ta_ref[...], wa_ref[...], preferred_element_type=F32)
    yb = jnp.dot(tb_ref[...], wb_ref[...], preferred_element_type=F32)
    o_ref[...] = (ga_ref[...].astype(F32) * ya + gb_ref[...].astype(F32) * yb).astype(o_ref.dtype)


def _mm_merge(t, wa, wb, sig, tm_pref=MM_ROWS, tn_pref=MM_COLS):
    m = t.shape[0]
    ka, n = wa.shape
    assert wb.shape == wa.shape and t.shape[1] == 2 * ka
    ko = sig.shape[1] - 2 * n
    tm, tn = _tile(m, tm_pref), _tile(n, tn_pref)
    off_a = ko // tn
    off_b = (ko + n) // tn
    assert off_a * tn == ko and off_b * tn == ko + n
    blk = 2 * _nbytes((tm, ka), BF16) + 2 * _nbytes((ka, tn), BF16) + 2 * _nbytes((tm, tn), sig.dtype) \
        + _nbytes((tm, tn), BF16)
    return pl.pallas_call(
        _mm_merge_kernel,
        grid=(m // tm, n // tn),
        in_specs=[pl.BlockSpec((tm, ka), lambda i, j: (i, 0)), pl.BlockSpec((tm, ka), lambda i, j: (i, 1)),
                  pl.BlockSpec((ka, tn), lambda i, j: (0, j)), pl.BlockSpec((ka, tn), lambda i, j: (0, j)),
                  pl.BlockSpec((tm, tn), lambda i, j: (i, j + off_a)),
                  pl.BlockSpec((tm, tn), lambda i, j: (i, j + off_b))],
        out_specs=pl.BlockSpec((tm, tn), lambda i, j: (i, j)),
        out_shape=jax.ShapeDtypeStruct((m, n), BF16),
        compiler_params=_params(2, blk, 3 * _nbytes((tm, tn), F32)), name="mm_merge",
    )(t, t, wa, wb, sig, sig)


def _ffn_up_kernel(h_ref, wg_ref, wu_ref, *rest, routed):
    h = h_ref[...]
    g = jnp.dot(h, wg_ref[...], preferred_element_type=F32)
    u = jnp.dot(h, wu_ref[...], preferred_element_type=F32)
    act = g * _sigmoid_tanh(g) * u
    if routed:
        gates = rest[0][...]
        lane = lax.broadcasted_iota(jnp.int32, gates.shape, 1)
        act = act * jnp.sum(jnp.where(lane == pl.program_id(1), gates, 0.0), axis=-1, keepdims=True)
    rest[-1][...] = act.astype(rest[-1].dtype)


def _ffn_up_dense(h, w13, ride=None, tm_pref=MM_ROWS, tn_pref=MM_COLS_NARROW):
    m, k = h.shape
    f = w13.shape[1] // 2
    tm, tn = _tile(m, tm_pref), _tile(f, tn_pref)
    nb = f // tn
    blk = _nbytes((tm, k), BF16) + 2 * _nbytes((k, tn), BF16) + _nbytes((tm, tn), BF16)
    in_specs = [pl.BlockSpec((tm, k), lambda i, j: (i, 0)), pl.BlockSpec((k, tn), lambda i, j: (0, j)),
                pl.BlockSpec((k, tn), lambda i, j: (0, j + nb))]
    return _pallas(functools.partial(_ffn_up_kernel, routed=False), (m // tm, nb), in_specs, [h, w13, w13],
                   [pl.BlockSpec((tm, tn), lambda i, j: (i, j))], [jax.ShapeDtypeStruct((m, f), BF16)],
                   blk, 3 * _nbytes((tm, tn), F32), "ffn_up_dense", ride)


def _ffn_up_moe(h, w13, gates, ride=None, tm_pref=MM_ROWS):
    m, k = h.shape
    two_de = w13.shape[2]
    de = two_de // 2
    tm = _tile(m, tm_pref)
    blk = _nbytes((tm, k), BF16) + 2 * _nbytes((k, de), BF16) + _nbytes((tm, de), BF16) + _nbytes((tm, LANES), F32)
    in_specs = [pl.BlockSpec((tm, k), lambda i, j: (i, 0)), pl.BlockSpec((None, k, de), lambda i, j: (j, 0, 0)),
                pl.BlockSpec((None, k, de), lambda i, j: (j, 0, 1)), pl.BlockSpec((tm, LANES), lambda i, j: (i, 0))]
    return _pallas(functools.partial(_ffn_up_kernel, routed=True), (m // tm, N_EXPERTS), in_specs, [h, w13, w13, gates],
                   [pl.BlockSpec((tm, de), lambda i, j: (i, j))], [jax.ShapeDtypeStruct((m, N_EXPERTS * de), BF16)],
                   blk, 3 * _nbytes((tm, de), F32), "ffn_up_moe", ride)


def _split3(x):
    hi = x.astype(BF16)
    r1 = x - hi.astype(F32)
    mid = r1.astype(BF16)
    lo = (r1 - mid.astype(F32)).astype(BF16)
    return hi, mid, lo


def _mlstm_step(qkv_f_ref, qkv_b_ref, pre_f_ref, pre_b_ref, bias_ref, hf_ref, hb_ref, c_ref, n_ref, m_ref):
    rows = lax.broadcasted_iota(jnp.int32, (CHUNK, CHUNK), 0)
    cols = lax.broadcasted_iota(jnp.int32, (CHUNK, CHUNK), 1)
    for d, (qkv_ref, pre_ref, out_ref) in enumerate(((qkv_f_ref, pre_f_ref, hf_ref), (qkv_b_ref, pre_b_ref, hb_ref))):
        mask = (rows >= cols) if d == 0 else (rows <= cols)
        tri = jnp.where(mask, 1.0, 0.0).astype(BF16)
        a = pre_ref[...] + bias_ref[...]
        a = GATE_SOFTCAP * jnp.tanh(a / GATE_SOFTCAP)
        lf = _log_sigmoid(a)
        hi, mid, lo = _split3(lf)
        b_all = (jnp.dot(tri, hi, preferred_element_type=F32) + jnp.dot(tri, mid, preferred_element_type=F32)
                 + jnp.dot(tri, lo, preferred_element_type=F32))
        a_t = a.T
        b_t = b_all.T
        last = CHUNK - 1 if d == 0 else 0
        for hd in range(M_HEADS):
            ci = d * 2 * M_HEADS + hd
            cf = ci + M_HEADS
            r = d * M_HEADS + hd
            ig_col, ig_row = a[:, ci:ci + 1], a_t[ci:ci + 1, :]
            b_col, b_row = b_all[:, cf:cf + 1], b_t[cf:cf + 1, :]
            b_last = b_col[last:last + 1, :]
            m_prev = m_ref[r:r + 1, 0:1]
            c_prev = c_ref[r]
            n_prev = n_ref[r:r + 1, :]
            q = qkv_ref[:, hd * M_DK:(hd + 1) * M_DK]
            k = qkv_ref[:, M_HEADS * M_DK + hd * M_DK:M_HEADS * M_DK + (hd + 1) * M_DK]
            v = qkv_ref[:, 2 * M_HEADS * M_DK + hd * M_DV:2 * M_HEADS * M_DK + (hd + 1) * M_DV]

            log_d = jnp.where(mask, b_col - b_row + ig_row, -jnp.inf)
            m_inter = b_col + m_prev
            m_row = jnp.maximum(m_inter, jnp.max(log_d, axis=-1, keepdims=True))
            qk = lax.dot_general(q, k, (((1,), (1,)), ((), ())), preferred_element_type=F32)
            s = qk * jnp.exp(log_d - m_row)
            w_inter = jnp.exp(m_inter - m_row)
            qc = jnp.dot(q, c_prev.astype(BF16), preferred_element_type=F32)
            num = jnp.dot(s.astype(BF16), v, preferred_element_type=F32) + w_inter * qc
            qn = jnp.sum(q.astype(F32) * n_prev, axis=-1, keepdims=True)
            den = jnp.sum(s, axis=-1, keepdims=True) + w_inter * qn
            out_ref[:, hd * M_DV:(hd + 1) * M_DV] = num / jnp.maximum(jnp.abs(den), jnp.exp(-m_row))

            log_w = b_last - b_col + ig_col
            m_new = jnp.maximum(b_last + m_prev, jnp.max(log_w, axis=0, keepdims=True))
            w = jnp.exp(log_w - m_new)
            decay = jnp.exp(b_last + m_prev - m_new)
            wk = w * k.astype(F32)
            c_ref[r] = decay * c_prev + lax.dot_general(wk.astype(BF16), v, (((0,), (0,)), ((), ())),
                                                        preferred_element_type=F32)
            n_ref[r:r + 1, :] = decay * n_prev + jnp.sum(wk, axis=0, keepdims=True)
            m_ref[r:r + 1, :] = jnp.broadcast_to(m_new, (1, LANES))


SCAN_STATE_SHAPES = ((N_DIRS * M_HEADS, M_DK, M_DV), (N_DIRS * M_HEADS, M_DK), (N_DIRS * M_HEADS, LANES))


def _mlstm_kernel(qkv_f_ref, qkv_b_ref, pre_f_ref, pre_b_ref, bias_ref, hf_ref, hb_ref, c_ref, n_ref, m_ref):
    @pl.when(pl.program_id(0) == 0)
    def _():
        c_ref[...] = jnp.zeros_like(c_ref)
        n_ref[...] = jnp.zeros_like(n_ref)
        m_ref[...] = jnp.zeros_like(m_ref)

    _mlstm_step(qkv_f_ref, qkv_b_ref, pre_f_ref, pre_b_ref, bias_ref, hf_ref, hb_ref, c_ref, n_ref, m_ref)


def _mlstm(qkv, pre, bias, l, n_ctx):
    rows, width = qkv.shape
    tr = CHUNK
    nc = rows // tr
    ncc = n_ctx // tr
    hw = M_HEADS * M_DV
    fwd = lambda t: (t, 0)
    bwd = lambda t: (jnp.where(t < ncc, ncc - 1 - t, ncc + nc - 1 - t), 0)
    blk = 2 * _nbytes((tr, width), BF16) + 2 * _nbytes((tr, LANES), F32) + 2 * _nbytes((tr, hw), F32)
    return pl.pallas_call(
        _mlstm_kernel,
        grid=(nc,),
        in_specs=[pl.BlockSpec((tr, width), fwd), pl.BlockSpec((tr, width), bwd),
                  pl.BlockSpec((tr, LANES), fwd), pl.BlockSpec((tr, LANES), bwd),
                  pl.BlockSpec((None, 1, LANES), lambda t: (l, 0, 0))],
        out_specs=[pl.BlockSpec((tr, hw), fwd), pl.BlockSpec((tr, hw), bwd)],
        out_shape=[jax.ShapeDtypeStruct((rows, hw), F32), jax.ShapeDtypeStruct((rows, hw), F32)],
        scratch_shapes=[pltpu.VMEM(s, F32) for s in SCAN_STATE_SHAPES],
        compiler_params=_params(1, blk, SCAN_TEMP_BYTES), name="mlstm_scan",
    )(qkv, qkv, pre, pre, bias)


def _branch_kernel(gu_ref, hf_ref, hb_ref, osig_ref, sgu_g_ref, sgu_w_ref, sgu_bt_ref, mh_g_ref, o_ref, *, a_width):
    gd = a_width // A_GROUPS
    for r0 in range(0, gu_ref.shape[0], CHUNK):
        rs = slice(r0, r0 + CHUNK)
        v = gu_ref[rs, a_width:2 * a_width].astype(F32)
        vc = v - jnp.mean(v, axis=-1, keepdims=True)
        vn = (vc * lax.rsqrt(jnp.mean(vc * vc, axis=-1, keepdims=True) + EPS) * sgu_g_ref[...]).astype(BF16)
        for g in range(A_GROUPS):
            sl = slice(g * gd, (g + 1) * gd)
            s = jnp.dot(sgu_w_ref[g].astype(BF16), vn[:, sl], preferred_element_type=F32) + sgu_bt_ref[:, g:g + 1]
            o_ref[rs, sl] = (gu_ref[rs, sl].astype(F32) * s).astype(o_ref.dtype)
        for hd in range(M_HEADS):
            sl = slice(hd * M_DV, (hd + 1) * M_DV)
            h = hf_ref[rs, sl] + hb_ref[rs, sl]
            hn = h * lax.rsqrt(jnp.mean(h * h, axis=-1, keepdims=True) + EPS)
            o_ref[rs, a_width + hd * M_DV:a_width + (hd + 1) * M_DV] = \
                (hn * mh_g_ref[:, sl] * osig_ref[rs, sl].astype(F32)).astype(o_ref.dtype)


def _branch_inputs(gu, hf, hb, sig, sgu_g, sgu_w, sgu_bt, mh_g, l, a_width):
    n = gu.shape[0]
    hw = M_HEADS * M_DV
    tr = _tile(n, STREAM_ROWS, unit=CHUNK)
    blk = _nbytes((tr, 2 * a_width), gu.dtype) + 2 * _nbytes((tr, hw), F32) + _nbytes((tr, hw), sig.dtype) \
        + _nbytes((tr, a_width + hw), BF16)
    return pl.pallas_call(
        functools.partial(_branch_kernel, a_width=a_width),
        grid=(n // tr,),
        in_specs=[pl.BlockSpec((tr, 2 * a_width), lambda t: (t, 0)),
                  pl.BlockSpec((tr, hw), lambda t: (t, 0)), pl.BlockSpec((tr, hw), lambda t: (t, 0)),
                  pl.BlockSpec((tr, hw), lambda t: (t, 0)),
                  pl.BlockSpec((None, 1, a_width), lambda t: (l, 0, 0)),
                  pl.BlockSpec((None, A_GROUPS, CHUNK, CHUNK), lambda t: (l, 0, 0, 0)),
                  pl.BlockSpec((None, CHUNK, LANES), lambda t: (l, 0, 0)),
                  pl.BlockSpec((None, 1, hw), lambda t: (l, 0, 0))],
        out_specs=pl.BlockSpec((tr, a_width + hw), lambda t: (t, 0)),
        out_shape=jax.ShapeDtypeStruct((n, a_width + hw), BF16),
        compiler_params=_params(1, blk, BRANCH_TEMP_BYTES), name="branch_inputs",
    )(gu, hf, hb, sig, sgu_g, sgu_w, sgu_bt, mh_g)


def _final_norm_kernel(x_ref, g_ref, o_ref):
    tm, d = x_ref.shape
    cc = min(COL_CHUNK, d)
    for r0 in range(0, tm, ROW_GROUP):
        inv = _inv_rms(x_ref, r0, ROW_GROUP)
        for c0 in range(0, d, cc):
            cs = slice(c0, c0 + cc)
            o_ref[r0:r0 + ROW_GROUP, cs] = x_ref[r0:r0 + ROW_GROUP, cs] * inv * g_ref[:, cs]


def _final_norm(x, g, n_ctx):
    m, d = x.shape
    tm = _tile(n_ctx, 256, unit=ROW_GROUP)
    skip = n_ctx // tm
    assert skip * tm == n_ctx and (m - n_ctx) % tm == 0
    return pl.pallas_call(
        _final_norm_kernel, grid=((m - n_ctx) // tm,),
        in_specs=[pl.BlockSpec((tm, d), lambda i: (i + skip, 0)), pl.BlockSpec((1, d), lambda i: (0, 0))],
        out_specs=pl.BlockSpec((tm, d), lambda i: (i, 0)),
        out_shape=jax.ShapeDtypeStruct((m - n_ctx, d), F32),
        compiler_params=_params(1, 2 * _nbytes((tm, d), F32), 2 * _nbytes((tm, d), F32)), name="final_norm",
    )(x, g)


def kernel(x, c, ctx, c_ctx, ada_w, ada_b, norm_mix, w_in, sgu_norm, sgu_w, sgu_b, gate_bias, mh_norm, w_br_a, w_br_b, w_out, norm_ffn, dense_w13, dense_w2, moe_router, moe_w13, moe_w2, final_norm):
    bsz, seq, d = x.shape
    n_ctx = ctx.shape[1]
    assert bsz == 1 and seq % CHUNK == 0 and n_ctx % CHUNK == 0
    depth = w_in.shape[0]
    a_width = sgu_norm.shape[1]
    qk_w = M_HEADS * M_DK
    hw = M_HEADS * M_DV
    n_main = 2 * a_width + 2 * qk_w + 2 * hw
    n_gate = 2 * d
    assert mh_norm.shape[1] == hw and a_width == hw and w_in.shape[2] == n_main + IF_WIDTH + n_gate
    o_q, o_o = 2 * a_width, 2 * a_width + 2 * qk_w + hw

    wt_in = jnp.swapaxes(w_in, 1, 2)
    moe_w13_rows = moe_w13.reshape(moe_w13.shape[0], -1, moe_w13.shape[3])
    moe_w2_rows = moe_w2.reshape(moe_w2.shape[0], -1, d)

    def ffn_src(l):
        return (dense_w13, dense_w2, l // 2) if l % 2 == 0 else (moe_w13_rows, moe_w2_rows, l // 2)

    w_if = _cast_w_if(wt_in, n_main)
    router = jnp.pad(moe_router, ((0, 0), (0, 0), (0, LANES - N_EXPERTS))).astype(BF16)
    cond = jnp.zeros((MOD_ROWS, d), F32).at[0].set(c[0]).at[1].set(c_ctx)
    mod = _adaln(cond, ada_w, ada_b)

    q_scale = jnp.concatenate([jnp.full((1, qk_w), M_DK ** -0.5, F32), jnp.ones((1, qk_w + hw), F32)], axis=1)
    bias_if = jnp.pad(gate_bias, ((0, 0), (0, LANES - IF_WIDTH))).reshape(depth, 1, LANES)
    sgu_bt = jnp.pad(jnp.swapaxes(sgu_b, 1, 2), ((0, 0), (0, 0), (0, LANES - A_GROUPS)))
    g_mix, g_ffn = norm_mix.reshape(depth, 1, d), norm_ffn.reshape(depth, 1, d)
    g_sgu, g_mh = sgu_norm.reshape(depth, 1, a_width), mh_norm.reshape(depth, 1, hw)

    w13_src, w2_src, lf = ffn_src(0)
    wts = dict(w_in=_cast_w_in(wt_in, 0, n_main, n_gate), wa=_cast_bf16(w_br_a, 0), wb=_cast_bf16(w_br_b, 0),
               wo=_cast_bf16(w_out, 0), w13=_cast_bf16(w13_src, lf), w2=_cast_bf16(w2_src, lf))

    xs = jnp.concatenate([ctx.reshape(n_ctx, d), x.reshape(seq, d)], axis=0)
    for l in range(depth):
        nxt = {}
        more = l + 1 < depth
        if more:
            w13_src, w2_src, lf = ffn_src(l + 1)

        def ride(kind, *a):
            return (kind, *a) if more else None

        def take(outs, name):
            if more:
                nxt[name] = outs[1]
            return outs[0]

        h, pre = _norm_mod(xs, g_mix, mod, l, 0, n_ctx, project=w_if)
        w_main = wts["w_in"]
        gu = take(_mm_act(h, w_main, 0, 0, o_q, "gelu", BF16, ride=ride("plain", w_br_a, l + 1)), "wa")
        qkv = take(_mm_act(h, w_main, 0, o_q, o_o - o_q, "colscale", BF16, col_scale=q_scale,
                           ride=ride("plain", w_br_b, l + 1)), "wb")
        sig = take(_mm_act(h, w_main, 0, o_o, hw + n_gate, "sigmoid", BF16,
                           ride=ride("w_in", wt_in, l + 1, n_main, n_gate)), "w_in")
        hf, hb = _mlstm(qkv, pre, bias_if, l, n_ctx)
        t = _branch_inputs(gu, hf, hb, sig, g_sgu, sgu_w, sgu_bt, g_mh, l, a_width)
        z = _mm_merge(t, wts["wa"], wts["wb"], sig)
        xs = take(_mm_residual(z, wts["wo"], xs, mod, l, 2, n_ctx, ride=ride("plain", w_out, l + 1)), "wo")
        if l % 2 == 0:
            h2 = _norm_mod(xs, g_ffn, mod, l, 3, n_ctx)
            act = take(_ffn_up_dense(h2, wts["w13"], ride=ride("plain", w13_src, lf)), "w13")
        else:
            h2, gates = _norm_mod(xs, g_ffn, mod, l, 3, n_ctx, router=router[l // 2])
            w13 = wts["w13"].reshape(N_EXPERTS, d, -1)
            act = take(_ffn_up_moe(h2, w13, gates, ride=ride("plain", w13_src, lf)), "w13")
        xs = take(_mm_residual(act, wts["w2"], xs, mod, l, 5, n_ctx, ride=ride("plain", w2_src, lf)), "w2")
        wts = nxt
    return _final_norm(xs, final_norm.reshape(1, d), n_ctx).reshape(bsz, seq, d)
```

```python
import functools
import math

import jax
import jax.numpy as jnp
from jax import lax
from jax.experimental import pallas as pl
from jax.experimental.pallas import tpu as pltpu

F32 = jnp.float32
BF16 = jnp.bfloat16

CHUNK = 128
A_GROUPS = 4
M_HEADS = 4
M_DK = 128
M_DV = 256
N_DIRS = 2
IF_WIDTH = N_DIRS * 2 * M_HEADS
GATE_SOFTCAP = 15.0
N_EXPERTS = 8
EPS = 1e-6
MOD_ROWS = 8

LANES = 128
BF16_SUBLANES = 16
V7X_VMEM_BYTES = 64 * 1024 * 1024
MIB = 1024 * 1024
VMEM_LIMIT_CAP = V7X_VMEM_BYTES - 3 * MIB
VMEM_SLACK_BYTES = 8 * MIB

MM_ROWS = 1280
MM_COLS = 1024
MM_COLS_NARROW = 512
STREAM_ROWS = 640
CAST_ROWS = 512
CAST_BLOCK_BYTES = 8 * MIB
SCAN_TEMP_BYTES = 10 * MIB
BRANCH_TEMP_BYTES = 6 * MIB


def _params(n_grid_dims, block_bytes, temp_bytes=0):
    need = 2 * block_bytes + temp_bytes + VMEM_SLACK_BYTES
    return pltpu.CompilerParams(
        dimension_semantics=("arbitrary",) * n_grid_dims,
        vmem_limit_bytes=int(min(max(need, 16 * MIB), VMEM_LIMIT_CAP)))


def _nbytes(shape, dtype):
    return math.prod(shape) * jnp.dtype(dtype).itemsize


def _tile(n, pref, unit=LANES):
    if n <= pref:
        return n
    t = pref - pref % unit
    while n % t:
        t -= unit
    assert t > 0
    return t


def _sigmoid(x):
    return 1.0 / (1.0 + jnp.exp(-x))


def _sigmoid_tanh(x):
    return 0.5 + 0.5 * jnp.tanh(0.5 * x)


def _gelu(x):
    return 0.5 * x * (1.0 + jnp.tanh(0.7978845608028654 * (x + 0.044715 * (x * x * x))))


def _log_sigmoid(x):
    return jnp.minimum(x, 0.0) - jnp.log1p(jnp.exp(-jnp.abs(x)))


def _row_select(mod_ref, tile_rows, tile_index, n_ctx):
    rows = tile_index * tile_rows + lax.broadcasted_iota(jnp.int32, (tile_rows, 1), 0)
    return jnp.where(rows < n_ctx, mod_ref[1:2, :], mod_ref[0:1, :])


def _wspec(w, l, block, index):
    if w.ndim == 2:
        return pl.BlockSpec(block, index)
    return pl.BlockSpec((None,) + block, lambda *g: (l,) + index(*g))


def _cast_kernel(w_ref, o_ref):
    o_ref[...] = w_ref[...].astype(o_ref.dtype)


def _cast_bf16(w, l):
    _, r, c = w.shape
    tr = _tile(r, max(BF16_SUBLANES, CAST_BLOCK_BYTES // (4 * c)), unit=BF16_SUBLANES)
    return pl.pallas_call(
        _cast_kernel, grid=(r // tr,),
        in_specs=[pl.BlockSpec((None, tr, c), lambda i: (l, i, 0))],
        out_specs=pl.BlockSpec((tr, c), lambda i: (i, 0)),
        out_shape=jax.ShapeDtypeStruct((r, c), BF16),
        compiler_params=_params(1, _nbytes((tr, c), F32) + _nbytes((tr, c), BF16)), name="cast_bf16",
    )(w)


def _cast_w_if_kernel(w_ref, o_ref):
    o_ref[...] = jnp.zeros_like(o_ref)
    o_ref[:IF_WIDTH, :] = w_ref[...].astype(BF16)


def _cast_w_if(wt, n_main):
    n_l, _, d = wt.shape
    assert n_main % IF_WIDTH == 0
    return pl.pallas_call(
        _cast_w_if_kernel, grid=(n_l,),
        in_specs=[pl.BlockSpec((None, IF_WIDTH, d), lambda l: (l, n_main // IF_WIDTH, 0))],
        out_specs=pl.BlockSpec((None, LANES, d), lambda l: (l, 0, 0)),
        out_shape=jax.ShapeDtypeStruct((n_l, LANES, d), BF16),
        compiler_params=_params(1, _nbytes((LANES, d), F32)), name="cast_w_if",
    )(wt)


def _w_in_rows(a_ref, b_ref, o_ref, is_gate):
    tr = a_ref.shape[0]
    keep = tr - IF_WIDTH
    first = pl.multiple_of(jnp.where(is_gate, IF_WIDTH, 0), IF_WIDTH)
    o_ref[:keep, :] = a_ref[pl.ds(first, keep), :].astype(BF16)
    o_ref[keep:, :] = jnp.where(is_gate, b_ref[...], a_ref[keep:, :]).astype(BF16)


def _w_in_specs(wt, l, n_main, n_gate, tr, chunk_of):
    _, n, d = wt.shape
    assert n == n_main + IF_WIDTH + n_gate and n_main % tr == 0 and n_gate % tr == 0 and tr % IF_WIDTH == 0
    per = tr // IF_WIDTH
    ins = [pl.BlockSpec((None, tr, d), lambda *g: (l, chunk_of(*g), 0)),
           pl.BlockSpec((None, IF_WIDTH, d), lambda *g: (l, per * (chunk_of(*g) + 1), 0))]
    out = pl.BlockSpec((tr, d), lambda *g: (chunk_of(*g), 0))
    return ins, out, jax.ShapeDtypeStruct((n_main + n_gate, d), BF16)


def _cast_w_in_kernel(a_ref, b_ref, o_ref, *, main_chunks):
    _w_in_rows(a_ref, b_ref, o_ref, pl.program_id(0) >= main_chunks)


def _cast_w_in(wt, l, n_main, n_gate):
    tr = math.gcd(n_main, n_gate, CAST_ROWS)
    ins, out, shape = _w_in_specs(wt, l, n_main, n_gate, tr, lambda i: i)
    d = wt.shape[2]
    return pl.pallas_call(
        functools.partial(_cast_w_in_kernel, main_chunks=n_main // tr), grid=((n_main + n_gate) // tr,),
        in_specs=ins, out_specs=out, out_shape=shape,
        compiler_params=_params(1, _nbytes((tr, d), F32) + _nbytes((tr, d), BF16)), name="cast_w_in",
    )(wt, wt)


def _ride_rows(r, n_steps):
    for cr in range(BF16_SUBLANES, r + 1, BF16_SUBLANES):
        if r % cr == 0 and r // cr <= n_steps:
            return cr
    return None


def _pallas(kern, grid, in_specs, args, out_specs, out_shapes, blk, tmp, name, ride=None):
    n_out = len(out_specs)

    def call(k, ins, outs, shapes, a, extra_blk=0):
        return pl.pallas_call(k, grid=grid, in_specs=ins, out_specs=outs, out_shape=shapes,
                              compiler_params=_params(len(grid), blk + extra_blk, tmp), name=name)(*a)

    if ride is None:
        return call(kern, in_specs, out_specs, out_shapes, args)
    n_steps = math.prod(grid)
    strides = [math.prod(grid[d + 1:]) for d in range(len(grid))]
    n_in = len(in_specs)

    def step(*g):
        return sum(gi * st for gi, st in zip(g, strides))

    if ride[0] == "plain":
        _, src, ls = ride
        _, r, c = src.shape
        cr = _ride_rows(r, n_steps)
        if cr is None:
            return (*call(kern, in_specs, out_specs, out_shapes, args), _cast_bf16(src, ls))
        last = r // cr - 1

        def chunk(*g):
            return jnp.minimum(step(*g), last)

        def riding(*refs):
            src_ref, dst_ref = refs[n_in], refs[n_in + 1 + n_out]
            dst_ref[...] = src_ref[...].astype(BF16)
            kern(*refs[:n_in], *refs[n_in + 1:n_in + 1 + n_out], *refs[n_in + 2 + n_out:])

        return call(riding, in_specs + [pl.BlockSpec((None, cr, c), lambda *g: (ls, chunk(*g), 0))],
                    out_specs + [pl.BlockSpec((cr, c), lambda *g: (chunk(*g), 0))],
                    out_shapes + [jax.ShapeDtypeStruct((r, c), BF16)], args + [src],
                    _nbytes((cr, c), F32) + _nbytes((cr, c), BF16))

    _, wt, ls, n_main, n_gate = ride
    d = wt.shape[2]
    tr = next((t for t in range(IF_WIDTH, n_main + 1, IF_WIDTH)
               if n_main % t == 0 and n_gate % t == 0 and (n_main + n_gate) // t <= n_steps), None)
    if tr is None:
        return (*call(kern, in_specs, out_specs, out_shapes, args), _cast_w_in(wt, ls, n_main, n_gate))
    last = (n_main + n_gate) // tr - 1
    main_chunks = n_main // tr

    def chunk(*g):
        return jnp.minimum(step(*g), last)

    ins, out, shape = _w_in_specs(wt, ls, n_main, n_gate, tr, chunk)

    def riding(*refs):
        a_ref, b_ref, dst_ref = refs[n_in], refs[n_in + 1], refs[n_in + 2 + n_out]
        ids = [pl.program_id(dd) for dd in range(len(grid))]
        _w_in_rows(a_ref, b_ref, dst_ref, chunk(*ids) >= main_chunks)
        kern(*refs[:n_in], *refs[n_in + 2:n_in + 2 + n_out], *refs[n_in + 3 + n_out:])

    return call(riding, in_specs + ins, out_specs + [out], out_shapes + [shape], args + [wt, wt],
                _nbytes((tr, d), F32) + _nbytes((tr, d), BF16))


def _adaln_kernel(cond_ref, w_ref, b_ref, o_ref):
    c = cond_ref[...]
    s = (c * _sigmoid(c)).astype(BF16)
    acc = jnp.dot(s, w_ref[...].astype(BF16), preferred_element_type=F32)
    o_ref[...] = acc + b_ref[...]


def _adaln(cond, ada_w, ada_b):
    n_layers, d, n = ada_w.shape
    r = cond.shape[0]
    tn = _tile(n, MM_COLS_NARROW)
    blk = _nbytes((d, tn), F32) + _nbytes((r, d), F32) + 2 * _nbytes((r, tn), F32)
    return pl.pallas_call(
        _adaln_kernel,
        grid=(n_layers, n // tn),
        in_specs=[pl.BlockSpec((r, d), lambda l, j: (0, 0)),
                  pl.BlockSpec((None, d, tn), lambda l, j: (l, 0, j)),
                  pl.BlockSpec((None, 1, tn), lambda l, j: (l, 0, j))],
        out_specs=pl.BlockSpec((None, r, tn), lambda l, j: (l, 0, j)),
        out_shape=jax.ShapeDtypeStruct((n_layers, r, n), F32),
        compiler_params=_params(2, blk, _nbytes((d, tn), BF16)), name="adaln",
    )(cond, ada_w, ada_b.reshape(n_layers, 1, n))


ROW_GROUP = 16
COL_CHUNK = 1024


def _inv_rms(x_ref, r0, rows):
    d = x_ref.shape[1]
    cc = min(COL_CHUNK, d)
    ss = None
    for c0 in range(0, d, cc):
        xc = x_ref[r0:r0 + rows, c0:c0 + cc]
        part = jnp.sum(xc * xc, axis=-1, keepdims=True)
        ss = part if ss is None else ss + part
    return lax.rsqrt(ss / d + EPS)


def _norm_mod_kernel(x_ref, g_ref, shift_ref, scale_ref, *rest, n_ctx, side):
    tm, d = x_ref.shape
    cc = min(COL_CHUNK, d)
    o_ref = rest[0] if side is None else rest[1]
    row0 = pl.program_id(0) * tm

    def normalise(row_of_group):
        for r0 in range(0, tm, ROW_GROUP):
            mr = row_of_group(r0)
            inv = _inv_rms(x_ref, r0, ROW_GROUP)
            for c0 in range(0, d, cc):
                cs = slice(c0, c0 + cc)
                y = x_ref[r0:r0 + ROW_GROUP, cs] * inv * g_ref[:, cs]
                o_ref[r0:r0 + ROW_GROUP, cs] = (y * (1.0 + scale_ref[pl.ds(mr, 1), cs])
                                                + shift_ref[pl.ds(mr, 1), cs]).astype(BF16)

    @pl.when(row0 >= n_ctx)
    def _():
        normalise(lambda r0: 0)

    @pl.when(row0 < n_ctx)
    def _():
        normalise(lambda r0: jnp.where(row0 + r0 < n_ctx, 1, 0))
    if side is None:
        return
    p_ref, _, side_ref = rest
    if side == "project":
        side_ref[...] = lax.dot_general(o_ref[...], p_ref[...], (((1,), (1,)), ((), ())), preferred_element_type=F32)
        return
    gate_ref = side_ref
    logits = jnp.dot(o_ref[...], p_ref[...], preferred_element_type=F32)
    lane = lax.broadcasted_iota(jnp.int32, logits.shape, 1)
    logits = jnp.where(lane < N_EXPERTS, logits, -jnp.inf)
    m1 = jnp.max(logits, axis=-1, keepdims=True)
    i1 = jnp.min(jnp.where(logits == m1, lane, LANES), axis=-1, keepdims=True)
    rest_l = jnp.where(lane == i1, -jnp.inf, logits)
    m2 = jnp.max(rest_l, axis=-1, keepdims=True)
    i2 = jnp.min(jnp.where(rest_l == m2, lane, LANES), axis=-1, keepdims=True)
    e2 = jnp.exp(m2 - m1)
    gate_ref[...] = jnp.where(lane == i1, 1.0 / (1.0 + e2), 0.0) + jnp.where(lane == i2, e2 / (1.0 + e2), 0.0)


def _norm_mod(x, g, mod, l, k_shift, n_ctx, router=None, project=None):
    m, d = x.shape
    tm = _tile(m, STREAM_ROWS, unit=ROW_GROUP)
    assert n_ctx % ROW_GROUP == 0 and m % tm == 0
    in_specs = [pl.BlockSpec((tm, d), lambda i: (i, 0)),
                pl.BlockSpec((None, 1, d), lambda i: (l, 0, 0)),
                pl.BlockSpec((None, MOD_ROWS, d), lambda i: (l, 0, k_shift)),
                pl.BlockSpec((None, MOD_ROWS, d), lambda i: (l, 0, k_shift + 1))]
    blk = _nbytes((tm, d), F32) + _nbytes((tm, d), BF16)
    tmp = 3 * _nbytes((tm, d), F32)
    if router is None and project is None:
        return pl.pallas_call(
            functools.partial(_norm_mod_kernel, n_ctx=n_ctx, side=None), grid=(m // tm,), in_specs=in_specs,
            out_specs=pl.BlockSpec((tm, d), lambda i: (i, 0)),
            out_shape=jax.ShapeDtypeStruct((m, d), BF16),
            compiler_params=_params(1, blk, tmp), name="norm_mod",
        )(x, g, mod, mod)
    if router is not None:
        side, p, p_spec = "router", router, pl.BlockSpec((d, LANES), lambda i: (0, 0))
    else:
        side, p, p_spec = "project", project, pl.BlockSpec((None, LANES, d), lambda i: (l, 0, 0))
    return pl.pallas_call(
        functools.partial(_norm_mod_kernel, n_ctx=n_ctx, side=side), grid=(m // tm,),
        in_specs=in_specs + [p_spec],
        out_specs=[pl.BlockSpec((tm, d), lambda i: (i, 0)), pl.BlockSpec((tm, LANES), lambda i: (i, 0))],
        out_shape=[jax.ShapeDtypeStruct((m, d), BF16), jax.ShapeDtypeStruct((m, LANES), F32)],
        compiler_params=_params(1, blk + _nbytes((d, LANES), BF16), tmp), name="norm_mod_" + side,
    )(x, g, mod, mod, p)


def _mm_act_kernel(x_ref, wt_ref, *rest, act):
    acc = lax.dot_general(x_ref[...], wt_ref[...], (((1,), (1,)), ((), ())), preferred_element_type=F32)
    if act == "gelu":
        acc = _gelu(acc)
    elif act == "sigmoid":
        acc = _sigmoid_tanh(acc)
    elif act == "colscale":
        acc = acc * rest[0][...]
    rest[-1][...] = acc.astype(rest[-1].dtype)


def _mm_act(x, wt, l, row0, n, act, out_dtype, col_scale=None, ride=None, tm_pref=MM_ROWS, tn_pref=MM_COLS):
    m, k = x.shape
    tm, tn = _tile(m, tm_pref), _tile(n, tn_pref)
    assert row0 % tn == 0
    rb = row0 // tn
    in_specs = [pl.BlockSpec((tm, k), lambda i, j: (i, 0)), _wspec(wt, l, (tn, k), lambda i, j: (rb + j, 0))]
    args = [x, wt]
    if act == "colscale":
        in_specs.append(pl.BlockSpec((1, tn), lambda i, j: (0, j)))
        args.append(col_scale)
    blk = _nbytes((tm, k), BF16) + _nbytes((tn, k), BF16) + _nbytes((tm, tn), out_dtype)
    return _pallas(functools.partial(_mm_act_kernel, act=act), (m // tm, n // tn), in_specs, args,
                   [pl.BlockSpec((tm, tn), lambda i, j: (i, j))], [jax.ShapeDtypeStruct((m, n), out_dtype)],
                   blk, 2 * _nbytes((tm, tn), F32), "mm_" + act, ride)


def _mm_residual_kernel(z_ref, w_ref, x_ref, g_ref, o_ref, *, n_ctx):
    acc = jnp.dot(z_ref[...], w_ref[...], preferred_element_type=F32)
    gate = _row_select(g_ref, z_ref.shape[0], pl.program_id(0), n_ctx)
    o_ref[...] = x_ref[...] + gate * acc


def _mm_residual(z, w, x, mod, l_mod, k_gate, n_ctx, ride=None, tm_pref=MM_ROWS, tn_pref=MM_COLS_NARROW):
    m, k = z.shape
    n = w.shape[1]
    tm, tn = _tile(m, tm_pref), _tile(n, tn_pref)
    gb = k_gate * (n // tn)
    blk = _nbytes((tm, k), BF16) + _nbytes((k, tn), BF16) + 2 * _nbytes((tm, tn), F32)
    in_specs = [pl.BlockSpec((tm, k), lambda i, j: (i, 0)), pl.BlockSpec((k, tn), lambda i, j: (0, j)),
                pl.BlockSpec((tm, tn), lambda i, j: (i, j)),
                pl.BlockSpec((None, MOD_ROWS, tn), lambda i, j: (l_mod, 0, gb + j))]
    return _pallas(functools.partial(_mm_residual_kernel, n_ctx=n_ctx), (m // tm, n // tn), in_specs, [z, w, x, mod],
                   [pl.BlockSpec((tm, tn), lambda i, j: (i, j))], [jax.ShapeDtypeStruct((m, n), F32)],
                   blk, 2 * _nbytes((tm, tn), F32), "mm_residual", ride)


def _mm_merge_kernel(ta_ref, tb_ref, wa_ref, wb_ref, ga_ref, gb_ref, o_ref):
    ya = jnp.dot(ta_ref[...], wa_ref[...], preferred_element_type=F32)
    yb = jnp.dot(tb_ref[...], wb_ref[...], preferred_element_type=F32)
    o_ref[...] = (ga_ref[...].astype(F32) * ya + gb_ref[...].astype(F32) * yb).astype(o_ref.dtype)


def _mm_merge(t, wa, wb, sig, tm_pref=MM_ROWS, tn_pref=MM_COLS):
    m = t.shape[0]
    ka, n = wa.shape
    assert wb.shape == wa.shape and t.shape[1] == 2 * ka
    ko = sig.shape[1] - 2 * n
    tm, tn = _tile(m, tm_pref), _tile(n, tn_pref)
    off_a = ko // tn
    off_b = (ko + n) // tn
    assert off_a * tn == ko and off_b * tn == ko + n
    blk = 2 * _nbytes((tm, ka), BF16) + 2 * _nbytes((ka, tn), BF16) + 2 * _nbytes((tm, tn), sig.dtype) \
        + _nbytes((tm, tn), BF16)
    return pl.pallas_call(
        _mm_merge_kernel,
        grid=(m // tm, n // tn),
        in_specs=[pl.BlockSpec((tm, ka), lambda i, j: (i, 0)), pl.BlockSpec((tm, ka), lambda i, j: (i, 1)),
                  pl.BlockSpec((ka, tn), lambda i, j: (0, j)), pl.BlockSpec((ka, tn), lambda i, j: (0, j)),
                  pl.BlockSpec((tm, tn), lambda i, j: (i, j + off_a)),
                  pl.BlockSpec((tm, tn), lambda i, j: (i, j + off_b))],
        out_specs=pl.BlockSpec((tm, tn), lambda i, j: (i, j)),
        out_shape=jax.ShapeDtypeStruct((m, n), BF16),
        compiler_params=_params(2, blk, 3 * _nbytes((tm, tn), F32)), name="mm_merge",
    )(t, t, wa, wb, sig, sig)


def _ffn_up_kernel(h_ref, wg_ref, wu_ref, *rest, routed):
    h = h_ref[...]
    g = jnp.dot(h, wg_ref[...], preferred_element_type=F32)
    u = jnp.dot(h, wu_ref[...], preferred_element_type=F32)
    act = g * _sigmoid_tanh(g) * u
    if routed:
        gates = rest[0][...]
        lane = lax.broadcasted_iota(jnp.int32, gates.shape, 1)
        act = act * jnp.sum(jnp.where(lane == pl.program_id(1), gates, 0.0), axis=-1, keepdims=True)
    rest[-1][...] = act.astype(rest[-1].dtype)


def _ffn_up_dense(h, w13, ride=None, tm_pref=MM_ROWS, tn_pref=MM_COLS_NARROW):
    m, k = h.shape
    f = w13.shape[1] // 2
    tm, tn = _tile(m, tm_pref), _tile(f, tn_pref)
    nb = f // tn
    blk = _nbytes((tm, k), BF16) + 2 * _nbytes((k, tn), BF16) + _nbytes((tm, tn), BF16)
    in_specs = [pl.BlockSpec((tm, k), lambda i, j: (i, 0)), pl.BlockSpec((k, tn), lambda i, j: (0, j)),
                pl.BlockSpec((k, tn), lambda i, j: (0, j + nb))]
    return _pallas(functools.partial(_ffn_up_kernel, routed=False), (m // tm, nb), in_specs, [h, w13, w13],
                   [pl.BlockSpec((tm, tn), lambda i, j: (i, j))], [jax.ShapeDtypeStruct((m, f), BF16)],
                   blk, 3 * _nbytes((tm, tn), F32), "ffn_up_dense", ride)


def _ffn_up_moe(h, w13, gates, ride=None, tm_pref=MM_ROWS):
    m, k = h.shape
    two_de = w13.shape[2]
    de = two_de // 2
    tm = _tile(m, tm_pref)
    blk = _nbytes((tm, k), BF16) + 2 * _nbytes((k, de), BF16) + _nbytes((tm, de), BF16) + _nbytes((tm, LANES), F32)
    in_specs = [pl.BlockSpec((tm, k), lambda i, j: (i, 0)), pl.BlockSpec((None, k, de), lambda i, j: (j, 0, 0)),
                pl.BlockSpec((None, k, de), lambda i, j: (j, 0, 1)), pl.BlockSpec((tm, LANES), lambda i, j: (i, 0))]
    return _pallas(functools.partial(_ffn_up_kernel, routed=True), (m // tm, N_EXPERTS), in_specs, [h, w13, w13, gates],
                   [pl.BlockSpec((tm, de), lambda i, j: (i, j))], [jax.ShapeDtypeStruct((m, N_EXPERTS * de), BF16)],
                   blk, 3 * _nbytes((tm, de), F32), "ffn_up_moe", ride)


def _split3(x):
    hi = x.astype(BF16)
    r1 = x - hi.astype(F32)
    mid = r1.astype(BF16)
    lo = (r1 - mid.astype(F32)).astype(BF16)
    return hi, mid, lo


def _mlstm_step(qkv_f_ref, qkv_b_ref, pre_f_ref, pre_b_ref, bias_ref, hf_ref, hb_ref, c_ref, n_ref, m_ref):
    rows = lax.broadcasted_iota(jnp.int32, (CHUNK, CHUNK), 0)
    cols = lax.broadcasted_iota(jnp.int32, (CHUNK, CHUNK), 1)
    for d, (qkv_ref, pre_ref, out_ref) in enumerate(((qkv_f_ref, pre_f_ref, hf_ref), (qkv_b_ref, pre_b_ref, hb_ref))):
        mask = (rows >= cols) if d == 0 else (rows <= cols)
        tri = jnp.where(mask, 1.0, 0.0).astype(BF16)
        a = pre_ref[...] + bias_ref[...]
        a = GATE_SOFTCAP * jnp.tanh(a / GATE_SOFTCAP)
        lf = _log_sigmoid(a)
        hi, mid, lo = _split3(lf)
        b_all = (jnp.dot(tri, hi, preferred_element_type=F32) + jnp.dot(tri, mid, preferred_element_type=F32)
                 + jnp.dot(tri, lo, preferred_element_type=F32))
        a_t = a.T
        b_t = b_all.T
        last = CHUNK - 1 if d == 0 else 0
        for hd in range(M_HEADS):
            ci = d * 2 * M_HEADS + hd
            cf = ci + M_HEADS
            r = d * M_HEADS + hd
            ig_col, ig_row = a[:, ci:ci + 1], a_t[ci:ci + 1, :]
            b_col, b_row = b_all[:, cf:cf + 1], b_t[cf:cf + 1, :]
            b_last = b_col[last:last + 1, :]
            m_prev = m_ref[r:r + 1, 0:1]
            c_prev = c_ref[r]
            n_prev = n_ref[r:r + 1, :]
            q = qkv_ref[:, hd * M_DK:(hd + 1) * M_DK]
            k = qkv_ref[:, M_HEADS * M_DK + hd * M_DK:M_HEADS * M_DK + (hd + 1) * M_DK]
            v = qkv_ref[:, 2 * M_HEADS * M_DK + hd * M_DV:2 * M_HEADS * M_DK + (hd + 1) * M_DV]

            t_row = jnp.where(mask, ig_row - b_row, -jnp.inf)
            m_inter = b_col + m_prev
            m_row = jnp.maximum(m_inter, b_col + jnp.max(t_row, axis=-1, keepdims=True))
            qk = lax.dot_general(q, k, (((1,), (1,)), ((), ())), preferred_element_type=F32)
            s = qk * jnp.exp(t_row + (b_col - m_row))
            w_inter = jnp.exp(m_inter - m_row)
            qc = jnp.dot(q, c_prev.astype(BF16), preferred_element_type=F32)
            num = jnp.dot(s.astype(BF16), v, preferred_element_type=F32) + w_inter * qc
            qn = jnp.sum(q.astype(F32) * n_prev, axis=-1, keepdims=True)
            den = jnp.sum(s, axis=-1, keepdims=True) + w_inter * qn
            out_ref[:, hd * M_DV:(hd + 1) * M_DV] = num / jnp.maximum(jnp.abs(den), jnp.exp(-m_row))

            log_w = b_last - b_col + ig_col
            m_new = jnp.maximum(b_last + m_prev, jnp.max(log_w, axis=0, keepdims=True))
            w = jnp.exp(log_w - m_new)
            decay = jnp.exp(b_last + m_prev - m_new)
            wk = w * k.astype(F32)
            c_ref[r] = decay * c_prev + lax.dot_general(wk.astype(BF16), v, (((0,), (0,)), ((), ())),
                                                        preferred_element_type=F32)
            n_ref[r:r + 1, :] = decay * n_prev + jnp.sum(wk, axis=0, keepdims=True)
            m_ref[r:r + 1, :] = jnp.broadcast_to(m_new, (1, LANES))


SCAN_STATE_SHAPES = ((N_DIRS * M_HEADS, M_DK, M_DV), (N_DIRS * M_HEADS, M_DK), (N_DIRS * M_HEADS, LANES))


def _mlstm_kernel(qkv_f_ref, qkv_b_ref, pre_f_ref, pre_b_ref, bias_ref, hf_ref, hb_ref, c_ref, n_ref, m_ref):
    @pl.when(pl.program_id(0) == 0)
    def _():
        c_ref[...] = jnp.zeros_like(c_ref)
        n_ref[...] = jnp.zeros_like(n_ref)
        m_ref[...] = jnp.zeros_like(m_ref)

    _mlstm_step(qkv_f_ref, qkv_b_ref, pre_f_ref, pre_b_ref, bias_ref, hf_ref, hb_ref, c_ref, n_ref, m_ref)


def _mlstm(qkv, pre, bias, l, n_ctx):
    rows, width = qkv.shape
    tr = CHUNK
    nc = rows // tr
    ncc = n_ctx // tr
    hw = M_HEADS * M_DV
    fwd = lambda t: (t, 0)
    bwd = lambda t: (jnp.where(t < ncc, ncc - 1 - t, ncc + nc - 1 - t), 0)
    blk = 2 * _nbytes((tr, width), BF16) + 2 * _nbytes((tr, LANES), F32) + 2 * _nbytes((tr, hw), F32)
    return pl.pallas_call(
        _mlstm_kernel,
        grid=(nc,),
        in_specs=[pl.BlockSpec((tr, width), fwd), pl.BlockSpec((tr, width), bwd),
                  pl.BlockSpec((tr, LANES), fwd), pl.BlockSpec((tr, LANES), bwd),
                  pl.BlockSpec((None, 1, LANES), lambda t: (l, 0, 0))],
        out_specs=[pl.BlockSpec((tr, hw), fwd), pl.BlockSpec((tr, hw), bwd)],
        out_shape=[jax.ShapeDtypeStruct((rows, hw), F32), jax.ShapeDtypeStruct((rows, hw), F32)],
        scratch_shapes=[pltpu.VMEM(s, F32) for s in SCAN_STATE_SHAPES],
        compiler_params=_params(1, blk, SCAN_TEMP_BYTES), name="mlstm_scan",
    )(qkv, qkv, pre, pre, bias)


def _branch_kernel(gu_ref, hf_ref, hb_ref, osig_ref, sgu_g_ref, sgu_w_ref, sgu_bt_ref, mh_g_ref, o_ref, *, a_width):
    gd = a_width // A_GROUPS
    for r0 in range(0, gu_ref.shape[0], CHUNK):
        rs = slice(r0, r0 + CHUNK)
        v = gu_ref[rs, a_width:2 * a_width].astype(F32)
        vc = v - jnp.mean(v, axis=-1, keepdims=True)
        vn = (vc * lax.rsqrt(jnp.mean(vc * vc, axis=-1, keepdims=True) + EPS) * sgu_g_ref[...]).astype(BF16)
        for g in range(A_GROUPS):
            sl = slice(g * gd, (g + 1) * gd)
            s = jnp.dot(sgu_w_ref[g].astype(BF16), vn[:, sl], preferred_element_type=F32) + sgu_bt_ref[:, g:g + 1]
            o_ref[rs, sl] = (gu_ref[rs, sl].astype(F32) * s).astype(o_ref.dtype)
        for hd in range(M_HEADS):
            sl = slice(hd * M_DV, (hd + 1) * M_DV)
            h = hf_ref[rs, sl] + hb_ref[rs, sl]
            hn = h * lax.rsqrt(jnp.mean(h * h, axis=-1, keepdims=True) + EPS)
            o_ref[rs, a_width + hd * M_DV:a_width + (hd + 1) * M_DV] = \
                (hn * mh_g_ref[:, sl] * osig_ref[rs, sl].astype(F32)).astype(o_ref.dtype)


def _branch_inputs(gu, hf, hb, sig, sgu_g, sgu_w, sgu_bt, mh_g, l, a_width):
    n = gu.shape[0]
    hw = M_HEADS * M_DV
    tr = _tile(n, STREAM_ROWS, unit=CHUNK)
    blk = _nbytes((tr, 2 * a_width), gu.dtype) + 2 * _nbytes((tr, hw), F32) + _nbytes((tr, hw), sig.dtype) \
        + _nbytes((tr, a_width + hw), BF16)
    return pl.pallas_call(
        functools.partial(_branch_kernel, a_width=a_width),
        grid=(n // tr,),
        in_specs=[pl.BlockSpec((tr, 2 * a_width), lambda t: (t, 0)),
                  pl.BlockSpec((tr, hw), lambda t: (t, 0)), pl.BlockSpec((tr, hw), lambda t: (t, 0)),
                  pl.BlockSpec((tr, hw), lambda t: (t, 0)),
                  pl.BlockSpec((None, 1, a_width), lambda t: (l, 0, 0)),
                  pl.BlockSpec((None, A_GROUPS, CHUNK, CHUNK), lambda t: (l, 0, 0, 0)),
                  pl.BlockSpec((None, CHUNK, LANES), lambda t: (l, 0, 0)),
                  pl.BlockSpec((None, 1, hw), lambda t: (l, 0, 0))],
        out_specs=pl.BlockSpec((tr, a_width + hw), lambda t: (t, 0)),
        out_shape=jax.ShapeDtypeStruct((n, a_width + hw), BF16),
        compiler_params=_params(1, blk, BRANCH_TEMP_BYTES), name="branch_inputs",
    )(gu, hf, hb, sig, sgu_g, sgu_w, sgu_bt, mh_g)


def _final_norm_kernel(x_ref, g_ref, o_ref):
    tm, d = x_ref.shape
    cc = min(COL_CHUNK, d)
    for r0 in range(0, tm, ROW_GROUP):
        inv = _inv_rms(x_ref, r0, ROW_GROUP)
        for c0 in range(0, d, cc):
            cs = slice(c0, c0 + cc)
            o_ref[r0:r0 + ROW_GROUP, cs] = x_ref[r0:r0 + ROW_GROUP, cs] * inv * g_ref[:, cs]


def _final_norm(x, g, n_ctx):
    m, d = x.shape
    tm = _tile(n_ctx, 256, unit=ROW_GROUP)
    skip = n_ctx // tm
    assert skip * tm == n_ctx and (m - n_ctx) % tm == 0
    return pl.pallas_call(
        _final_norm_kernel, grid=((m - n_ctx) // tm,),
        in_specs=[pl.BlockSpec((tm, d), lambda i: (i + skip, 0)), pl.BlockSpec((1, d), lambda i: (0, 0))],
        out_specs=pl.BlockSpec((tm, d), lambda i: (i, 0)),
        out_shape=jax.ShapeDtypeStruct((m - n_ctx, d), F32),
        compiler_params=_params(1, 2 * _nbytes((tm, d), F32), 2 * _nbytes((tm, d), F32)), name="final_norm",
    )(x, g)


def kernel(x, c, ctx, c_ctx, ada_w, ada_b, norm_mix, w_in, sgu_norm, sgu_w, sgu_b, gate_bias, mh_norm, w_br_a, w_br_b, w_out, norm_ffn, dense_w13, dense_w2, moe_router, moe_w13, moe_w2, final_norm):
    bsz, seq, d = x.shape
    n_ctx = ctx.shape[1]
    assert bsz == 1 and seq % CHUNK == 0 and n_ctx % CHUNK == 0
    depth = w_in.shape[0]
    a_width = sgu_norm.shape[1]
    qk_w = M_HEADS * M_DK
    hw = M_HEADS * M_DV
    n_main = 2 * a_width + 2 * qk_w + 2 * hw
    n_gate = 2 * d
    assert mh_norm.shape[1] == hw and a_width == hw and w_in.shape[2] == n_main + IF_WIDTH + n_gate
    o_q, o_o = 2 * a_width, 2 * a_width + 2 * qk_w + hw

    wt_in = jnp.swapaxes(w_in, 1, 2)
    moe_w13_rows = moe_w13.reshape(moe_w13.shape[0], -1, moe_w13.shape[3])
    moe_w2_rows = moe_w2.reshape(moe_w2.shape[0], -1, d)

    def ffn_src(l):
        return (dense_w13, dense_w2, l // 2) if l % 2 == 0 else (moe_w13_rows, moe_w2_rows, l // 2)

    w_if = _cast_w_if(wt_in, n_main)
    router = jnp.pad(moe_router, ((0, 0), (0, 0), (0, LANES - N_EXPERTS))).astype(BF16)
    cond = jnp.zeros((MOD_ROWS, d), F32).at[0].set(c[0]).at[1].set(c_ctx)
    mod = _adaln(cond, ada_w, ada_b)

    q_scale = jnp.concatenate([jnp.full((1, qk_w), M_DK ** -0.5, F32), jnp.ones((1, qk_w + hw), F32)], axis=1)
    bias_if = jnp.pad(gate_bias, ((0, 0), (0, LANES - IF_WIDTH))).reshape(depth, 1, LANES)
    sgu_bt = jnp.pad(jnp.swapaxes(sgu_b, 1, 2), ((0, 0), (0, 0), (0, LANES - A_GROUPS)))
    g_mix, g_ffn = norm_mix.reshape(depth, 1, d), norm_ffn.reshape(depth, 1, d)
    g_sgu, g_mh = sgu_norm.reshape(depth, 1, a_width), mh_norm.reshape(depth, 1, hw)

    w13_src, w2_src, lf = ffn_src(0)
    wts = dict(w_in=_cast_w_in(wt_in, 0, n_main, n_gate), wa=_cast_bf16(w_br_a, 0), wb=_cast_bf16(w_br_b, 0),
               wo=_cast_bf16(w_out, 0), w13=_cast_bf16(w13_src, lf), w2=_cast_bf16(w2_src, lf))

    xs = jnp.concatenate([ctx.reshape(n_ctx, d), x.reshape(seq, d)], axis=0)
    for l in range(depth):
        nxt = {}
        more = l + 1 < depth
        if more:
            w13_src, w2_src, lf = ffn_src(l + 1)

        def ride(kind, *a):
            return (kind, *a) if more else None

        def take(outs, name):
            if more:
                nxt[name] = outs[1]
            return outs[0]

        h, pre = _norm_mod(xs, g_mix, mod, l, 0, n_ctx, project=w_if)
        w_main = wts["w_in"]
        gu = take(_mm_act(h, w_main, 0, 0, o_q, "gelu", BF16, ride=ride("plain", w_br_a, l + 1)), "wa")
        qkv = take(_mm_act(h, w_main, 0, o_q, o_o - o_q, "colscale", BF16, col_scale=q_scale,
                           ride=ride("plain", w_br_b, l + 1)), "wb")
        sig = take(_mm_act(h, w_main, 0, o_o, hw + n_gate, "sigmoid", BF16,
                           ride=ride("w_in", wt_in, l + 1, n_main, n_gate)), "w_in")
        hf, hb = _mlstm(qkv, pre, bias_if, l, n_ctx)
        t = _branch_inputs(gu, hf, hb, sig, g_sgu, sgu_w, sgu_bt, g_mh, l, a_width)
        z = _mm_merge(t, wts["wa"], wts["wb"], sig)
        xs = take(_mm_residual(z, wts["wo"], xs, mod, l, 2, n_ctx, ride=ride("plain", w_out, l + 1)), "wo")
        if l % 2 == 0:
            h2 = _norm_mod(xs, g_ffn, mod, l, 3, n_ctx)
            act = take(_ffn_up_dense(h2, wts["w13"], ride=ride("plain", w13_src, lf)), "w13")
        else:
            h2, gates = _norm_mod(xs, g_ffn, mod, l, 3, n_ctx, router=router[l // 2])
            w13 = wts["w13"].reshape(N_EXPERTS, d, -1)
            act = take(_ffn_up_moe(h2, w13, gates, ride=ride("plain", w13_src, lf)), "w13")
        xs = take(_mm_residual(act, wts["w2"], xs, mod, l, 5, n_ctx, ride=ride("plain", w2_src, lf)), "w2")
        wts = nxt
    return _final_norm(xs, final_norm.reshape(1, d), n_ctx).reshape(bsz, seq, d)
```
